```python
import math
import jax
import jax.numpy as jnp
from jax import lax
import numpy as np

D_MODEL = 2048
BATCH = 4
SEQ = 2048
DEPTH = 2

GRID_W = 64
CTX_LEN = 256
EPS = 1e-6
POS_BASE = 10000.0
N_MOD = 6
CHUNK = 128
GMLP_DFF = 6 * D_MODEL
GMLP_HALF = GMLP_DFF // 2
GMLP_GROUPS = 8
SSD_INNER = 2 * D_MODEL
SSD_HEADDIM = 64
SSD_HEADS = SSD_INNER // SSD_HEADDIM
SSD_STATE = 128
SSD_GROUPS = 8
SSD_CONV = 5
SSD_CHUNK = 128
SSD_CONV_CH = SSD_INNER + 2 * SSD_GROUPS * SSD_STATE
SSD_IN_DIM = SSD_INNER + SSD_CONV_CH + 2 * SSD_HEADS
FFN_DFF = 5632
N_EXPERTS = 8
TOP_K = 2
MOE_DFF = 7168

kernel_name = 'hybrid_gmlp_ssd_moe_diffusion_trunk'

F32 = jnp.float32


def rmsnorm(x, g):
    xf = x.astype(F32)
    y = xf * lax.rsqrt(jnp.mean(xf * xf, axis=-1, keepdims=True) + EPS)
    return (y * g.astype(F32)).astype(x.dtype)


def layernorm(x, g, b):
    xf = x.astype(F32)
    mu = jnp.mean(xf, axis=-1, keepdims=True)
    var = jnp.mean(jnp.square(xf - mu), axis=-1, keepdims=True)
    y = (xf - mu) * lax.rsqrt(var + EPS)
    return (y * g.astype(F32) + b.astype(F32)).astype(x.dtype)


def modulate(h, shift, scale):
    return h * (1 + scale) + shift


def ada_modulation(cvec, w, b):
    m = jax.nn.silu(cvec) @ w + b
    return jnp.split(m[:, None, :], N_MOD, axis=-1)


def sincos_pos_2d(rows, dtype):
    quarter = D_MODEL // 4
    half = D_MODEL // 2
    omega = 1.0 / (POS_BASE ** (jnp.arange(quarter, dtype=F32) / quarter))

    def emb1d(n):
        ang = jnp.arange(n, dtype=F32)[:, None] * omega[None, :]
        return jnp.concatenate([jnp.sin(ang), jnp.cos(ang)], axis=-1)

    er = emb1d(rows)
    ec = emb1d(GRID_W)
    grid = jnp.concatenate([jnp.broadcast_to(er[:, None, :], (rows, GRID_W, half)),
                            jnp.broadcast_to(ec[None, :, :], (rows, GRID_W, half))], axis=-1)
    return grid.reshape(rows * GRID_W, D_MODEL).astype(dtype)


def sgu_mixer(h, p):
    b, l, _ = h.shape
    z = jax.nn.gelu(h @ p['gmlp_in_w'] + p['gmlp_in_b'], approximate=False)
    u, v = jnp.split(z, 2, axis=-1)
    v = layernorm(v, p['sgu_ln_g'], p['sgu_ln_b'])
    v = v.reshape(b, l // CHUNK, CHUNK, GMLP_GROUPS, GMLP_HALF // GMLP_GROUPS)
    s = jnp.einsum('gqk,bnkgc->bnqgc', p['sgu_ws'], v) + p['sgu_bs'].T[:, :, None]
    return (u * s.reshape(b, l, GMLP_HALF)) @ p['gmlp_out_w']


def centred_dwconv(x, w, b):
    ch = x.shape[-1]
    y = lax.conv_general_dilated(x, w[:, None, :].astype(x.dtype), window_strides=(1,),
                                 padding=[(SSD_CONV // 2, SSD_CONV // 2)],
                                 dimension_numbers=('NWC', 'WIO', 'NWC'), feature_group_count=ch)
    return y + b


def ssd_project(h, p):
    b, l, _ = h.shape
    proj = h @ p['ssd_in_w']
    z, xbc, dt = jnp.split(proj, [SSD_INNER, SSD_INNER + SSD_CONV_CH], axis=-1)
    xbc = jax.nn.silu(centred_dwconv(xbc, p['conv_w'], p['conv_b']))
    xs, bm, cm = jnp.split(xbc, [SSD_INNER, SSD_INNER + SSD_GROUPS * SSD_STATE], axis=-1)
    xs = xs.reshape(b, l, SSD_HEADS, SSD_HEADDIM)
    bm = bm.reshape(b, l, SSD_GROUPS, SSD_STATE)
    cm = cm.reshape(b, l, SSD_GROUPS, SSD_STATE)
    dt = jax.nn.softplus(dt.astype(F32).reshape(b, l, 2, SSD_HEADS) + p['dt_bias'].astype(F32))
    return z, xs, bm, cm, dt


def segsum(a):
    q = a.shape[-1]
    cs = jnp.cumsum(a, axis=-1)
    diff = cs[..., :, None] - cs[..., None, :]
    mask = jnp.tril(jnp.ones((q, q), dtype=bool))
    return jnp.where(mask, diff, -jnp.inf)


def ssd_chunked(xs, dt, a_coef, bm, cm, init):
    b, l, h, pd = xs.shape
    g, n, q = SSD_GROUPS, SSD_STATE, SSD_CHUNK
    r = h // g
    c = l // q
    X = (xs.astype(F32) * dt[..., None]).reshape(b, c, q, g, r, pd)
    a = jnp.transpose((dt * a_coef).reshape(b, c, q, g, r), (0, 3, 4, 1, 2))
    a_cum = jnp.cumsum(a, axis=-1)
    bc = bm.astype(F32).reshape(b, c, q, g, n)
    cc = cm.astype(F32).reshape(b, c, q, g, n)
    decay_in = jnp.exp(segsum(a))
    y_diag = jnp.einsum('bclgn,bcsgn,bgrcls,bcsgrp->bclgrp', cc, bc, decay_in, X)
    decay_states = jnp.exp(a_cum[..., -1:] - a_cum)
    states = jnp.einsum('bcsgn,bgrcs,bcsgrp->bcgrpn', bc, decay_states, X)
    chunk_decay = jnp.exp(a_cum[..., -1])

    def step(carry, inp):
        st, dec = inp
        return carry * dec[..., None, None] + st, carry

    final, prev = lax.scan(step, init, (jnp.moveaxis(states, 1, 0), jnp.moveaxis(chunk_decay, -1, 0)))
    prev = jnp.moveaxis(prev, 0, 1)
    y_off = jnp.einsum('bclgn,bcgrpn,bgrcl->bclgrp', cc, prev, jnp.exp(a_cum))
    return (y_diag + y_off).reshape(b, l, h, pd), final


def ssd_final_state(xs, dt, a_coef, bm, init):
    b, l, h, pd = xs.shape
    g = SSD_GROUPS
    r = h // g
    X = (xs.astype(F32) * dt[..., None]).reshape(b, l, g, r, pd)
    a_glob = jnp.cumsum((dt * a_coef).reshape(b, l, g, r), axis=1)
    decay = jnp.exp(a_glob[:, -1:] - a_glob)
    return (jnp.exp(a_glob[:, -1])[..., None, None] * init
            + jnp.einsum('blgn,blgr,blgrp->bgrpn', bm.astype(F32), decay, X))


def gated_out(y, z, p):
    b, l = y.shape[:2]
    gs = SSD_INNER // SSD_GROUPS
    yf = y.reshape(b, l, SSD_GROUPS, gs) * jax.nn.silu(z.astype(F32)).reshape(b, l, SSD_GROUPS, gs)
    yf = yf * lax.rsqrt(jnp.mean(yf * yf, axis=-1, keepdims=True) + EPS)
    yf = yf.reshape(b, l, SSD_INNER) * p['ssd_norm_g'].astype(F32)
    return yf.astype(z.dtype) @ p['ssd_out_w']


def ssd_mixer(h_lat, h_ctx, p, ctx_out_needed):
    a_coef = -jnp.exp(p['a_log'].astype(F32))
    d_skip = p['d_skip'].astype(F32)[:, None]
    z_l, x_l, b_l, c_l, dt_l = ssd_project(h_lat, p)
    z_c, x_c, b_c, c_c, dt_c = ssd_project(h_ctx, p)
    init0 = jnp.zeros((h_lat.shape[0], SSD_GROUPS, SSD_HEADS // SSD_GROUPS, SSD_HEADDIM, SSD_STATE), F32)
    y_l = d_skip * x_l.astype(F32)
    y_c = d_skip * x_c.astype(F32) if ctx_out_needed else None
    for d in range(2):
        rev = d == 1
        xl, bl, cl, dtl = x_l, b_l, c_l, dt_l[:, :, d]
        xc, bcx, ccx, dtc = x_c, b_c, c_c, dt_c[:, :, d]
        if rev:
            xl, bl, cl, dtl = (jnp.flip(t, axis=1) for t in (xl, bl, cl, dtl))
            xc, bcx, ccx, dtc = (jnp.flip(t, axis=1) for t in (xc, bcx, ccx, dtc))
        if ctx_out_needed:
            yc_d, s_ctx = ssd_chunked(xc, dtc, a_coef[d], bcx, ccx, init0)
            y_c = y_c + (jnp.flip(yc_d, axis=1) if rev else yc_d)
        else:
            s_ctx = ssd_final_state(xc, dtc, a_coef[d], bcx, init0)
        yl_d, _ = ssd_chunked(xl, dtl, a_coef[d], bl, cl, s_ctx)
        y_l = y_l + (jnp.flip(yl_d, axis=1) if rev else yl_d)
    out_l = gated_out(y_l, z_l, p)
    out_c = gated_out(y_c, z_c, p) if ctx_out_needed else None
    return out_l, out_c


def swiglu(h, w_gu, w_down):
    g, u = jnp.split(h @ w_gu, 2, axis=-1)
    return (jax.nn.silu(g) * u) @ w_down


def moe_ffn(h, router_w, w_gu, w_down):
    logits = (h @ router_w).astype(F32)
    top_vals, top_idx = lax.top_k(logits, TOP_K)
    top_w = jax.nn.softmax(top_vals, axis=-1)
    gates = jnp.sum(jax.nn.one_hot(top_idx, N_EXPERTS, dtype=F32) * top_w[..., None], axis=-2)
    out = jnp.zeros(h.shape, F32)
    for e in range(N_EXPERTS):
        out = out + gates[..., e:e + 1] * swiglu(h, w_gu[e], w_down[e]).astype(F32)
    return out.astype(h.dtype)


def channel_mixer(h, p, use_moe):
    if use_moe:
        return moe_ffn(h, p['router_w'], p['moe_w_gu'], p['moe_w_down'])
    return swiglu(h, p['ffn_w_gu'], p['ffn_w_down'])


def setup_inputs(seed: int = 0) -> dict:
    key = jax.random.key(seed)
    keys = list(jax.random.split(key, 48))

    def nrm(shape, scale):
        return scale * jax.random.normal(keys.pop(), shape, F32)

    def gain(n):
        return 1.0 + nrm((n,), 0.05)

    D = D_MODEL
    dt0 = jnp.exp(jax.random.uniform(keys.pop(), (2, SSD_HEADS), F32, math.log(1e-3), math.log(1e-1)))
    dt_bias = dt0 + jnp.log(-jnp.expm1(-dt0))
    a_log = jnp.log(jax.random.uniform(keys.pop(), (2, SSD_HEADS), F32, 1.0, 16.0))
    return {
        'x': nrm((BATCH, SEQ, D), 1.0),
        'c': nrm((BATCH, D), 1.0),
        'ctx': nrm((BATCH, CTX_LEN, D), 1.0),
        'c_ctx': nrm((D,), 1.0),
        'l0_ada_w': nrm((D, N_MOD * D), 0.5 * D ** -0.5),
        'l0_ada_b': nrm((N_MOD * D,), 0.02),
        'l0_norm_mix': gain(D),
        'l0_gmlp_in_w': nrm((D, GMLP_DFF), D ** -0.5),
        'l0_gmlp_in_b': nrm((GMLP_DFF,), 0.02),
        'l0_sgu_ln_g': gain(GMLP_HALF),
        'l0_sgu_ln_b': nrm((GMLP_HALF,), 0.02),
        'l0_sgu_ws': nrm((GMLP_GROUPS, CHUNK, CHUNK), CHUNK ** -0.5),
        'l0_sgu_bs': 1.0 + nrm((GMLP_GROUPS, CHUNK), 0.05),
        'l0_gmlp_out_w': nrm((GMLP_HALF, D), GMLP_HALF ** -0.5),
        'l0_norm_ffn': gain(D),
        'l0_ffn_w_gu': nrm((D, 2 * FFN_DFF), D ** -0.5),
        'l0_ffn_w_down': nrm((FFN_DFF, D), FFN_DFF ** -0.5),
        'l1_ada_w': nrm((D, N_MOD * D), 0.5 * D ** -0.5),
        'l1_ada_b': nrm((N_MOD * D,), 0.02),
        'l1_norm_mix': gain(D),
        'l1_ssd_in_w': nrm((D, SSD_IN_DIM), D ** -0.5),
        'l1_conv_w': nrm((SSD_CONV, SSD_CONV_CH), SSD_CONV ** -0.5),
        'l1_conv_b': nrm((SSD_CONV_CH,), 0.02),
        'l1_dt_bias': dt_bias,
        'l1_a_log': a_log,
        'l1_d_skip': 1.0 + nrm((SSD_HEADS,), 0.1),
        'l1_ssd_norm_g': gain(SSD_INNER),
        'l1_ssd_out_w': nrm((SSD_INNER, D), SSD_INNER ** -0.5),
        'l1_norm_ffn': gain(D),
        'l1_router_w': nrm((D, N_EXPERTS), D ** -0.5),
        'l1_moe_w_gu': nrm((N_EXPERTS, D, 2 * MOE_DFF), D ** -0.5),
        'l1_moe_w_down': nrm((N_EXPERTS, MOE_DFF, D), MOE_DFF ** -0.5),
        'final_norm': gain(D),
    }


def reference(x, c, ctx, c_ctx,
              l0_ada_w, l0_ada_b, l0_norm_mix, l0_gmlp_in_w, l0_gmlp_in_b, l0_sgu_ln_g, l0_sgu_ln_b,
              l0_sgu_ws, l0_sgu_bs, l0_gmlp_out_w, l0_norm_ffn, l0_ffn_w_gu, l0_ffn_w_down,
              l1_ada_w, l1_ada_b, l1_norm_mix, l1_ssd_in_w, l1_conv_w, l1_conv_b, l1_dt_bias, l1_a_log,
              l1_d_skip, l1_ssd_norm_g, l1_ssd_out_w, l1_norm_ffn, l1_router_w, l1_moe_w_gu, l1_moe_w_down,
              final_norm):
    layers = (
        dict(ada_w=l0_ada_w, ada_b=l0_ada_b, norm_mix=l0_norm_mix, gmlp_in_w=l0_gmlp_in_w,
             gmlp_in_b=l0_gmlp_in_b, sgu_ln_g=l0_sgu_ln_g, sgu_ln_b=l0_sgu_ln_b, sgu_ws=l0_sgu_ws,
             sgu_bs=l0_sgu_bs, gmlp_out_w=l0_gmlp_out_w, norm_ffn=l0_norm_ffn, ffn_w_gu=l0_ffn_w_gu,
             ffn_w_down=l0_ffn_w_down),
        dict(ada_w=l1_ada_w, ada_b=l1_ada_b, norm_mix=l1_norm_mix, ssd_in_w=l1_ssd_in_w, conv_w=l1_conv_w,
             conv_b=l1_conv_b, dt_bias=l1_dt_bias, a_log=l1_a_log, d_skip=l1_d_skip, ssd_norm_g=l1_ssd_norm_g,
             ssd_out_w=l1_ssd_out_w, norm_ffn=l1_norm_ffn, router_w=l1_router_w, moe_w_gu=l1_moe_w_gu,
             moe_w_down=l1_moe_w_down),
    )
    rows = x.shape[1] // GRID_W
    x = x + sincos_pos_2d(rows, x.dtype)[None]
    x_c = ctx
    for i in range(DEPTH):
        p = layers[i]
        use_ssd = i % 2 == 1
        ctx_out_needed = i < DEPTH - 1
        sh1, sc1, g1, sh2, sc2, g2 = ada_modulation(c, p['ada_w'], p['ada_b'])
        csh1, csc1, cg1, csh2, csc2, cg2 = ada_modulation(c_ctx[None], p['ada_w'], p['ada_b'])
        h_l = modulate(rmsnorm(x, p['norm_mix']), sh1, sc1)
        h_c = modulate(rmsnorm(x_c, p['norm_mix']), csh1, csc1)
        if use_ssd:
            o_l, o_c = ssd_mixer(h_l, h_c, p, ctx_out_needed)
        else:
            o_l = sgu_mixer(h_l, p)
            o_c = sgu_mixer(h_c, p) if ctx_out_needed else None
        x = x + g1 * o_l
        x = x + g2 * channel_mixer(modulate(rmsnorm(x, p['norm_ffn']), sh2, sc2), p, use_ssd)
        if ctx_out_needed:
            x_c = x_c + cg1 * o_c
            x_c = x_c + cg2 * channel_mixer(modulate(rmsnorm(x_c, p['norm_ffn']), csh2, csc2), p, use_ssd)
    return rmsnorm(x, final_norm)
```

```python
import functools
import math

import jax
import jax.numpy as jnp
from jax import lax
from jax.experimental import pallas as pl
from jax.experimental.pallas import tpu as pltpu

F32 = jnp.float32
BF16 = jnp.bfloat16
I32 = jnp.int32

D_MODEL = 2048
GRID_W = 64
EPS = 1e-6
POS_BASE = 10000.0
N_MOD = 6
CHUNK = 128
GMLP_DFF = 6 * D_MODEL
GMLP_HALF = GMLP_DFF // 2
GMLP_GROUPS = 8
GMLP_GC = GMLP_HALF // GMLP_GROUPS
SSD_INNER = 2 * D_MODEL
SSD_HEADDIM = 64
SSD_HEADS = SSD_INNER // SSD_HEADDIM
SSD_STATE = 128
SSD_GROUPS = 8
SSD_RPG = SSD_HEADS // SSD_GROUPS
SSD_GW = SSD_RPG * SSD_HEADDIM
SSD_CONV = 5
SSD_CHUNK = 128
SSD_BC = SSD_GROUPS * SSD_STATE
SSD_CONV_CH = SSD_INNER + 2 * SSD_BC
SSD_ZX = SSD_INNER + SSD_CONV_CH
FFN_DFF = 5632
N_EXPERTS = 8
TOP_K = 2
MOE_DFF = 7168

LANES = 128
INV_SQRT2 = 0.7071067811865476
NEG_BIG = -1e30
VMEM_LIMIT = 56 * 1024 * 1024

TM_DENSE = 512
TM_WIDE = 1024
TM_MOE = 1024
TF_MOE = 256
TG = 512
TC = 256


def _cparams(*sem):
    return pltpu.CompilerParams(dimension_semantics=sem, vmem_limit_bytes=VMEM_LIMIT)


def _silu(x):
    return x * (1.0 / (1.0 + jnp.exp(-x)))


def _dot(a, b):
    return jnp.dot(a, b, preferred_element_type=F32)


def _dot_nt(a, b):
    return lax.dot_general(a, b, (((1,), (1,)), ((), ())), preferred_element_type=F32)


def _dot_tn(a, b):
    return lax.dot_general(a, b, (((0,), (0,)), ((), ())), preferred_element_type=F32)


def _rms_mod(x, gain, shift, scale):
    ms = jnp.mean(x * x, axis=-1, keepdims=True)
    y = x * lax.rsqrt(ms + EPS) * gain
    return y * (1.0 + scale) + shift


def _mod_part(m_ref, k):
    return m_ref[0, :, k * D_MODEL:(k + 1) * D_MODEL]


def _split3(v):
    hi = v.astype(BF16)
    r1 = v - hi.astype(F32)
    mid = r1.astype(BF16)
    lo = (r1 - mid.astype(F32)).astype(BF16)
    return hi, mid, lo


def _ada_kernel(c_ref, w_ref, b_ref, o_ref):
    s = _silu(c_ref[...]).astype(BF16)
    o_ref[...] = _dot(s, w_ref[...].astype(BF16)) + b_ref[...]


def _ada(cvec, w, b):
    n = w.shape[1]
    tn = 1024
    out = pl.pallas_call(
        _ada_kernel,
        grid=(n // tn,),
        in_specs=[pl.BlockSpec((8, D_MODEL), lambda j: (0, 0)),
                  pl.BlockSpec((D_MODEL, tn), lambda j: (0, j)),
                  pl.BlockSpec((1, tn), lambda j: (0, j))],
        out_specs=pl.BlockSpec((8, tn), lambda j: (0, j)),
        out_shape=jax.ShapeDtypeStruct((8, n), F32),
        compiler_params=_cparams("arbitrary"),
        name="ada",
    )(cvec, w, b.reshape(1, n))
    return out.reshape(8, 1, n)


def _prep_kernel(x_ref, c_ref, pos_ref, g_ref, m_ref, xr_ref, h_ref, *, n_lat):
    i = pl.program_id(0)
    sh, sc = _mod_part(m_ref, 0), _mod_part(m_ref, 1)

    def emit(xv):
        xr_ref[...] = xv
        h_ref[...] = _rms_mod(xv, g_ref[...], sh, sc).astype(BF16)

    @pl.when(i < n_lat)
    def _():
        emit(x_ref[...] + pos_ref[...])

    @pl.when(i >= n_lat)
    def _():
        emit(c_ref[...])


def _mod_row_map(n_lat, tiles_per_batch, ctx_row):
    def row(i):
        return jnp.where(i < n_lat, i // tiles_per_batch, ctx_row)
    return row


def _prep(x2, c2, pos, gain, mods, batch, seq):
    tm = TM_DENSE
    n_lat, n_ctx = x2.shape[0] // tm, c2.shape[0] // tm
    tpb = seq // tm
    row = _mod_row_map(n_lat, tpb, batch)
    t = x2.shape[0] + c2.shape[0]
    return pl.pallas_call(
        functools.partial(_prep_kernel, n_lat=n_lat),
        grid=(n_lat + n_ctx,),
        in_specs=[pl.BlockSpec((tm, D_MODEL), lambda i: (jnp.minimum(i, n_lat - 1), 0)),
                  pl.BlockSpec((tm, D_MODEL), lambda i: (jnp.maximum(i - n_lat, 0), 0)),
                  pl.BlockSpec((tm, D_MODEL), lambda i: (i % tpb, 0)),
                  pl.BlockSpec((1, D_MODEL), lambda i: (0, 0)),
                  pl.BlockSpec((1, 1, N_MOD * D_MODEL), lambda i: (row(i), 0, 0))],
        out_specs=[pl.BlockSpec((tm, D_MODEL), lambda i: (i, 0)),
                   pl.BlockSpec((tm, D_MODEL), lambda i: (i, 0))],
        out_shape=[jax.ShapeDtypeStruct((t, D_MODEL), F32),
                   jax.ShapeDtypeStruct((t, D_MODEL), BF16)],
        compiler_params=_cparams("arbitrary"),
        name="prep",
    )(x2, c2, pos, gain, mods)


def _gmlp_in_kernel(h_ref, w_ref, b_ref, z_ref, st_ref, *, n_u, tn, cw):
    j = pl.program_id(1)
    h = h_ref[...]
    tm = h.shape[0]
    s1 = jnp.zeros((tm, LANES), F32)
    s2 = jnp.zeros((tm, LANES), F32)
    for c in range(tn // cw):
        a = _dot(h, w_ref[:, c * cw:(c + 1) * cw].astype(BF16)) + b_ref[:, c * cw:(c + 1) * cw]
        g = 0.5 * a * (1.0 + lax.erf(a * INV_SQRT2))
        z_ref[:, c * cw:(c + 1) * cw] = g.astype(BF16)
        for q in range(cw // LANES):
            gq = g[:, q * LANES:(q + 1) * LANES]
            s1 = s1 + gq
            s2 = s2 + gq * gq

    @pl.when(j == n_u)
    def _():
        st_ref[:, 0:LANES] = s1
        st_ref[:, LANES:2 * LANES] = s2

    @pl.when(j > n_u)
    def _():
        st_ref[:, 0:LANES] += s1
        st_ref[:, LANES:2 * LANES] += s2


def _gmlp_in(h, w, b):
    t = h.shape[0]
    tm, tn = TM_WIDE, 1024
    n = w.shape[1]
    return pl.pallas_call(
        functools.partial(_gmlp_in_kernel, n_u=GMLP_HALF // tn, tn=tn, cw=256),
        grid=(t // tm, n // tn),
        in_specs=[pl.BlockSpec((tm, D_MODEL), lambda i, j: (i, 0)),
                  pl.BlockSpec((D_MODEL, tn), lambda i, j: (0, j)),
                  pl.BlockSpec((1, tn), lambda i, j: (0, j))],
        out_specs=[pl.BlockSpec((tm, tn), lambda i, j: (i, j)),
                   pl.BlockSpec((tm, 2 * LANES), lambda i, j: (i, 0))],
        out_shape=[jax.ShapeDtypeStruct((t, n), BF16),
                   jax.ShapeDtypeStruct((t, 2 * LANES), F32)],
        compiler_params=_cparams("arbitrary", "arbitrary"),
        name="gmlp_in",
    )(h, w, b.reshape(1, n))


def _sgu_out_kernel(zu_ref, zv_ref, st_ref, lg_ref, lb_ref, ws_ref, bs_ref, wo_ref, xr_ref,
                    m_ref, gn_ref, x1_ref, hn_ref, gs_ref, *, n_k):
    k = pl.program_id(1)
    tm = zu_ref.shape[0]
    st = st_ref[...]
    mu = jnp.sum(st[:, 0:LANES], axis=-1, keepdims=True) * (1.0 / GMLP_HALF)
    ex2 = jnp.sum(st[:, LANES:2 * LANES], axis=-1, keepdims=True) * (1.0 / GMLP_HALF)
    rstd = lax.rsqrt(ex2 - mu * mu + EPS)
    ws = ws_ref[0].astype(BF16)
    bs = bs_ref[0]
    for r in range(tm // CHUNK):
        rows = slice(r * CHUNK, (r + 1) * CHUNK)
        v = zv_ref[rows, :].astype(F32)
        vn = (v - mu[rows]) * rstd[rows] * lg_ref[...] + lb_ref[...]
        s = _dot(ws, vn.astype(BF16)) + bs
        gs_ref[rows, :] = (zu_ref[rows, :].astype(F32) * s).astype(BF16)
    part = _mod_part(m_ref, 2) * _dot(gs_ref[...], wo_ref[...])

    @pl.when(k == 0)
    def _():
        x1_ref[...] = xr_ref[...] + part

    @pl.when(k > 0)
    def _():
        x1_ref[...] += part

    @pl.when(k == n_k - 1)
    def _():
        hn_ref[...] = _rms_mod(x1_ref[...], gn_ref[...], _mod_part(m_ref, 3),
                               _mod_part(m_ref, 4)).astype(BF16)


def _sgu_out(z, st, lg, lb, ws, bs, wo, xr, mods, gain_next, row):
    t = xr.shape[0]
    tm, n_k = TM_DENSE, GMLP_GROUPS
    return pl.pallas_call(
        functools.partial(_sgu_out_kernel, n_k=n_k),
        grid=(t // tm, n_k),
        in_specs=[pl.BlockSpec((tm, GMLP_GC), lambda i, k: (i, k)),
                  pl.BlockSpec((tm, GMLP_GC), lambda i, k: (i, n_k + k)),
                  pl.BlockSpec((tm, 2 * LANES), lambda i, k: (i, 0)),
                  pl.BlockSpec((1, GMLP_GC), lambda i, k: (0, k)),
                  pl.BlockSpec((1, GMLP_GC), lambda i, k: (0, k)),
                  pl.BlockSpec((1, CHUNK, CHUNK), lambda i, k: (k, 0, 0)),
                  pl.BlockSpec((1, CHUNK, 1), lambda i, k: (k, 0, 0)),
                  pl.BlockSpec((GMLP_GC, D_MODEL), lambda i, k: (k, 0)),
                  pl.BlockSpec((tm, D_MODEL), lambda i, k: (i, 0)),
                  pl.BlockSpec((1, 1, N_MOD * D_MODEL), lambda i, k: (row(i), 0, 0)),
                  pl.BlockSpec((1, D_MODEL), lambda i, k: (0, 0))],
        out_specs=[pl.BlockSpec((tm, D_MODEL), lambda i, k: (i, 0)),
                   pl.BlockSpec((tm, D_MODEL), lambda i, k: (i, 0))],
        out_shape=[jax.ShapeDtypeStruct((t, D_MODEL), F32),
                   jax.ShapeDtypeStruct((t, D_MODEL), BF16)],
        scratch_shapes=[pltpu.VMEM((tm, GMLP_GC), BF16)],
        compiler_params=_cparams("arbitrary", "arbitrary"),
        name="sgu_out",
    )(z, z, st, lg, lb, ws, bs, wo, xr, mods, gain_next)


def _ffn_kernel(h_ref, wg_ref, wu_ref, wd_ref, x1_ref, m_ref, mn_ref, gn_ref, x2_ref, hn_ref, *, n_f):
    f = pl.program_id(1)
    h = h_ref[...]
    a = (_silu(_dot(h, wg_ref[...])) * _dot(h, wu_ref[...])).astype(BF16)
    part = _mod_part(m_ref, 5) * _dot(a, wd_ref[...])

    @pl.when(f == 0)
    def _():
        x2_ref[...] = x1_ref[...] + part

    @pl.when(f > 0)
    def _():
        x2_ref[...] += part

    @pl.when(f == n_f - 1)
    def _():
        hn_ref[...] = _rms_mod(x2_ref[...], gn_ref[...], _mod_part(mn_ref, 0),
                               _mod_part(mn_ref, 1)).astype(BF16)


def _ffn(h, wgu, wd, x1, mods, mods_next, gain_next, row):
    t = h.shape[0]
    tm, tf = TM_DENSE, 512
    n_f = FFN_DFF // tf
    mspec = pl.BlockSpec((1, 1, N_MOD * D_MODEL), lambda i, f: (row(i), 0, 0))
    return pl.pallas_call(
        functools.partial(_ffn_kernel, n_f=n_f),
        grid=(t // tm, n_f),
        in_specs=[pl.BlockSpec((tm, D_MODEL), lambda i, f: (i, 0)),
                  pl.BlockSpec((D_MODEL, tf), lambda i, f: (0, f)),
                  pl.BlockSpec((D_MODEL, tf), lambda i, f: (0, n_f + f)),
                  pl.BlockSpec((tf, D_MODEL), lambda i, f: (f, 0)),
                  pl.BlockSpec((tm, D_MODEL), lambda i, f: (i, 0)),
                  mspec, mspec,
                  pl.BlockSpec((1, D_MODEL), lambda i, f: (0, 0))],
        out_specs=[pl.BlockSpec((tm, D_MODEL), lambda i, f: (i, 0)),
                   pl.BlockSpec((tm, D_MODEL), lambda i, f: (i, 0))],
        out_shape=[jax.ShapeDtypeStruct((t, D_MODEL), F32),
                   jax.ShapeDtypeStruct((t, D_MODEL), BF16)],
        compiler_params=_cparams("arbitrary", "arbitrary"),
        name="ffn",
    )(h, wgu, wgu, wd, x1, mods, mods_next, gain_next)


def _proj_kernel(h_ref, w_ref, o_ref, *, tn, cw):
    h = h_ref[...]
    for c in range(tn // cw):
        o_ref[:, c * cw:(c + 1) * cw] = _dot(
            h, w_ref[:, c * cw:(c + 1) * cw].astype(BF16)).astype(o_ref.dtype)


def _dt_kernel(h_ref, w_ref, b_ref, o_ref):
    a = _dot(h_ref[...], w_ref[...].astype(BF16)) + b_ref[...]
    o_ref[...] = jnp.maximum(a, 0.0) + jnp.log(1.0 + jnp.exp(-jnp.abs(a)))


def _ssd_in(h, w, dt_bias):
    t = h.shape[0]
    tm, tn = TM_WIDE, 1024
    zx = pl.pallas_call(
        functools.partial(_proj_kernel, tn=tn, cw=256),
        grid=(t // tm, SSD_ZX // tn),
        in_specs=[pl.BlockSpec((tm, D_MODEL), lambda i, j: (i, 0)),
                  pl.BlockSpec((D_MODEL, tn), lambda i, j: (0, j))],
        out_specs=pl.BlockSpec((tm, tn), lambda i, j: (i, j)),
        out_shape=jax.ShapeDtypeStruct((t, SSD_ZX), BF16),
        compiler_params=_cparams("arbitrary", "arbitrary"),
        name="ssd_in",
    )(h, w)
    nd = 2 * SSD_HEADS
    dt = pl.pallas_call(
        _dt_kernel,
        grid=(t // tm,),
        in_specs=[pl.BlockSpec((tm, D_MODEL), lambda i: (i, 0)),
                  pl.BlockSpec((D_MODEL, nd), lambda i: (0, SSD_ZX // nd)),
                  pl.BlockSpec((1, nd), lambda i: (0, 0))],
        out_specs=pl.BlockSpec((tm, nd), lambda i: (i, 0)),
        out_shape=jax.ShapeDtypeStruct((t, nd), F32),
        compiler_params=_cparams("arbitrary"),
        name="ssd_dt",
    )(h, w, dt_bias.reshape(1, nd))
    return zx, dt


def _conv_kernel(z_ref, w_ref, b_ref, o_ref, pad_ref, *, seq):
    tc = z_ref.shape[1]
    halo = 8
    pad_ref[0:halo, :] = jnp.zeros((halo, tc), F32)
    pad_ref[halo:halo + seq, :] = z_ref[...].astype(F32)
    pad_ref[halo + seq:2 * halo + seq, :] = jnp.zeros((halo, tc), F32)
    acc = jnp.zeros((seq, tc), F32) + b_ref[...]
    for k in range(SSD_CONV):
        off = halo - SSD_CONV // 2 + k
        acc = acc + pad_ref[off:off + seq, :] * w_ref[k:k + 1, :]
    o_ref[...] = _silu(acc).astype(BF16)


def _conv(zx, w, b, batch, seq, row0):
    tc = 256
    nblk = SSD_CONV_CH // tc
    off_r, off_c = row0 // seq, SSD_INNER // tc
    return pl.pallas_call(
        functools.partial(_conv_kernel, seq=seq),
        grid=(batch, nblk),
        in_specs=[pl.BlockSpec((seq, tc), lambda bi, c: (off_r + bi, off_c + c)),
                  pl.BlockSpec((SSD_CONV, tc), lambda bi, c: (0, c)),
                  pl.BlockSpec((1, tc), lambda bi, c: (0, c))],
        out_specs=pl.BlockSpec((seq, tc), lambda bi, c: (bi, c)),
        out_shape=jax.ShapeDtypeStruct((batch * seq, SSD_CONV_CH), BF16),
        scratch_shapes=[pltpu.VMEM((seq + 16, tc), F32)],
        compiler_params=_cparams("arbitrary", "arbitrary"),
        name="ssd_conv",
    )(zx, w, b.reshape(1, SSD_CONV_CH))


def _cumsum_rows(a, ltri3_ref):
    hi, mid, lo = _split3(a)
    return _dot(ltri3_ref[...], jnp.concatenate([hi, mid, lo], axis=0))


def _expand_heads(v, r3_ref):
    hi, mid, lo = _split3(v)
    return _dot(jnp.concatenate([hi, mid, lo], axis=1), r3_ref[...])


def _state_update(s_ref, x_f32, bmat, w1x, decay_row):
    xw = (x_f32 * w1x).astype(BF16)
    for g in range(SSD_GROUPS):
        gs = slice(g * SSD_GW, (g + 1) * SSD_GW)
        bg = bmat[:, g * SSD_STATE:(g + 1) * SSD_STATE]
        s_ref[g] = s_ref[g] * decay_row[:, gs] + _dot_tn(bg, xw[:, gs])


def _state_readout(s_ref, cmat, scale):
    outs = []
    for g in range(SSD_GROUPS):
        gs = slice(g * SSD_GW, (g + 1) * SSD_GW)
        cg = cmat[:, g * SSD_STATE:(g + 1) * SSD_STATE]
        outs.append(_dot(cg, s_ref[g].astype(BF16)) * scale[:, gs])
    return outs


def _ssd_fwd_kernel(xl_ref, bl_ref, cl_ref, xc_ref, bc_ref, dt_ref, alog_ref, dsk_ref, ltri_ref,
                    r3_ref, y_ref, s_ref, *, n_ctx):
    s = pl.program_id(1)
    q = SSD_CHUNK

    @pl.when(s == 0)
    def _():
        s_ref[...] = jnp.zeros(s_ref.shape, F32)

    dt = dt_ref[...]
    a = dt * (-jnp.exp(alog_ref[...]))
    cs = _cumsum_rows(a, ltri_ref)
    cs_end = cs[q - 1:q, :]
    w1 = dt * jnp.exp(cs_end - cs)
    ex = _expand_heads(jnp.concatenate([jnp.exp(cs), w1], axis=0), r3_ref)
    ef, w1x = ex[0:q], ex[q:2 * q]

    @pl.when(s < n_ctx)
    def _():
        _state_update(s_ref, xc_ref[...].astype(F32), bc_ref[...], w1x, ef[q - 1:q])

    @pl.when(s >= n_ctx)
    def _():
        ce = cs - a
        cst, cet, dtt = cs.T, ce.T, dt.T
        ii = lax.broadcasted_iota(I32, (q, q), 0)
        jj = lax.broadcasted_iota(I32, (q, q), 1)
        lower, upper = ii >= jj, ii <= jj
        lane = lax.broadcasted_iota(I32, (q, LANES), 1)
        x = xl_ref[...]
        bmat, cmat = bl_ref[...], cl_ref[...]
        xf = x.astype(F32)
        yoff = _state_readout(s_ref, cmat, ef)
        nh = SSD_HEADS
        for g in range(SSD_GROUPS):
            gmat = _dot_nt(cmat[:, g * SSD_STATE:(g + 1) * SSD_STATE],
                           bmat[:, g * SSD_STATE:(g + 1) * SSD_STATE])
            for pr in range(SSD_RPG // 2):
                col = g * SSD_GW + pr * LANES
                xp = x[:, col:col + LANES]
                acc = None
                for half in range(2):
                    h = g * SSD_RPG + pr * 2 + half
                    mf = jnp.exp(jnp.where(lower, cs[:, h:h + 1] - cst[h:h + 1, :], NEG_BIG))
                    mb = jnp.exp(jnp.where(upper, cet[nh + h:nh + h + 1, :] - ce[:, nh + h:nh + h + 1],
                                           NEG_BIG))
                    m = ((mf * dtt[h:h + 1, :] + mb * dtt[nh + h:nh + h + 1, :]) * gmat).astype(BF16)
                    keep = (lane < SSD_HEADDIM) if half == 0 else (lane >= SSD_HEADDIM)
                    d = _dot(m, jnp.where(keep, xp, jnp.zeros_like(xp)))
                    acc = d if acc is None else acc + d
                y_ref[:, col:col + LANES] = (
                    acc + yoff[g][:, pr * LANES:(pr + 1) * LANES]
                    + dsk_ref[:, col:col + LANES] * xf[:, col:col + LANES])
        _state_update(s_ref, xf, bmat, w1x, ef[q - 1:q])


def _ssd_bwd_kernel(xl_ref, bl_ref, cl_ref, xc_ref, bc_ref, dt_ref, alog_ref, ltri_ref, r3_ref,
                    y1_ref, z_ref, ng_ref, o_ref, s_ref, *, n_ctx):
    s = pl.program_id(1)
    q = SSD_CHUNK

    @pl.when(s == 0)
    def _():
        s_ref[...] = jnp.zeros(s_ref.shape, F32)

    dt = dt_ref[...]
    a = dt * (-jnp.exp(alog_ref[...]))
    cs = _cumsum_rows(a, ltri_ref)
    ce = cs - a
    cs_end = cs[q - 1:q, :]
    w1 = dt * jnp.exp(ce)
    ex = _expand_heads(jnp.concatenate([jnp.exp(cs_end - ce), w1], axis=0), r3_ref)
    eb, w1x = ex[0:q], ex[q:2 * q]

    @pl.when(s < n_ctx)
    def _():
        _state_update(s_ref, xc_ref[...].astype(F32), bc_ref[...], w1x, eb[0:1])

    @pl.when(s >= n_ctx)
    def _():
        x = xl_ref[...]
        yoff = _state_readout(s_ref, cl_ref[...], eb)
        for g in range(SSD_GROUPS):
            gs = slice(g * SSD_GW, (g + 1) * SSD_GW)
            yf = (y1_ref[:, gs] + yoff[g]) * _silu(z_ref[:, gs].astype(F32))
            ms = jnp.mean(yf * yf, axis=-1, keepdims=True)
            o_ref[:, gs] = (yf * lax.rsqrt(ms + EPS) * ng_ref[:, gs]).astype(BF16)
        _state_update(s_ref, x.astype(F32), bl_ref[...], w1x, eb[0:1])


def _ssd(xbc_l, xbc_c, dt, zx, a_log, d_skip, norm_g, batch, seq, seq_c):
    q = SSD_CHUNK
    n_lat, n_ctx = seq // q, seq_c // q
    steps = n_ctx + n_lat
    t_lat = batch * seq
    nd = 2 * SSD_HEADS
    cb = SSD_INNER // SSD_BC
    tri = (jnp.arange(q)[:, None] >= jnp.arange(q)[None, :]).astype(BF16)
    ltri3 = jnp.concatenate([tri, tri, tri], axis=1)
    chan_head = jnp.arange(SSD_INNER) // SSD_HEADDIM
    r_f = (jnp.arange(nd)[:, None] == chan_head[None, :]).astype(BF16)
    r_b = (jnp.arange(nd)[:, None] == chan_head[None, :] + SSD_HEADS).astype(BF16)
    r3_f = jnp.concatenate([r_f, r_f, r_f], axis=0)
    r3_b = jnp.concatenate([r_b, r_b, r_b], axis=0)
    alog = a_log.reshape(1, nd)
    dsk = jnp.repeat(d_skip, SSD_HEADDIM).reshape(1, SSD_INNER)

    def lat_f(s):
        return jnp.maximum(s - n_ctx, 0)

    def ctx_f(s):
        return jnp.minimum(s, n_ctx - 1)

    def lat_b(s):
        return n_lat - 1 - jnp.maximum(s - n_ctx, 0)

    def ctx_b(s):
        return n_ctx - 1 - jnp.minimum(s, n_ctx - 1)

    def specs(lat, ctx):
        def dt_row(b, s):
            return jnp.where(s < n_ctx, t_lat // q + b * n_ctx + ctx(s), b * n_lat + lat(s))
        return [pl.BlockSpec((q, SSD_INNER), lambda b, s: (b * n_lat + lat(s), 0)),
                pl.BlockSpec((q, SSD_BC), lambda b, s: (b * n_lat + lat(s), cb)),
                pl.BlockSpec((q, SSD_BC), lambda b, s: (b * n_lat + lat(s), cb + 1)),
                pl.BlockSpec((q, SSD_INNER), lambda b, s: (b * n_ctx + ctx(s), 0)),
                pl.BlockSpec((q, SSD_BC), lambda b, s: (b * n_ctx + ctx(s), cb)),
                pl.BlockSpec((q, nd), lambda b, s: (dt_row(b, s), 0)),
                pl.BlockSpec((1, nd), lambda b, s: (0, 0))]

    const = lambda shape: pl.BlockSpec(shape, lambda b, s: (0,) * len(shape))
    state = pltpu.VMEM((SSD_GROUPS, SSD_STATE, SSD_GW), F32)

    y1 = pl.pallas_call(
        functools.partial(_ssd_fwd_kernel, n_ctx=n_ctx),
        grid=(batch, steps),
        in_specs=specs(lat_f, ctx_f) + [const((1, SSD_INNER)), const((q, 3 * q)),
                                        const((3 * nd, SSD_INNER))],
        out_specs=pl.BlockSpec((q, SSD_INNER), lambda b, s: (b * n_lat + lat_f(s), 0)),
        out_shape=jax.ShapeDtypeStruct((t_lat, SSD_INNER), F32),
        scratch_shapes=[state],
        compiler_params=_cparams("arbitrary", "arbitrary"),
        name="ssd_fwd",
    )(xbc_l, xbc_l, xbc_l, xbc_c, xbc_c, dt, alog, dsk, ltri3, r3_f)

    yn = pl.pallas_call(
        functools.partial(_ssd_bwd_kernel, n_ctx=n_ctx),
        grid=(batch, steps),
        in_specs=specs(lat_b, ctx_b) + [
            const((q, 3 * q)), const((3 * nd, SSD_INNER)),
            pl.BlockSpec((q, SSD_INNER), lambda b, s: (b * n_lat + lat_b(s), 0)),
            pl.BlockSpec((q, SSD_INNER), lambda b, s: (b * n_lat + lat_b(s), 0)),
            const((1, SSD_INNER))],
        out_specs=pl.BlockSpec((q, SSD_INNER), lambda b, s: (b * n_lat + lat_b(s), 0)),
        out_shape=jax.ShapeDtypeStruct((t_lat, SSD_INNER), BF16),
        scratch_shapes=[state],
        compiler_params=_cparams("arbitrary", "arbitrary"),
        name="ssd_bwd",
    )(xbc_l, xbc_l, xbc_l, xbc_c, xbc_c, dt, alog, ltri3, r3_b, y1, zx,
      norm_g.reshape(1, SSD_INNER))
    return yn


def _ssd_out_kernel(y_ref, w_ref, x2_ref, m_ref, gn_ref, rw_ref, x3_ref, hm_ref, rt_ref, *, n_k):
    k = pl.program_id(1)
    part = _mod_part(m_ref, 2) * _dot(y_ref[...], w_ref[...])

    @pl.when(k == 0)
    def _():
        x3_ref[...] = x2_ref[...] + part

    @pl.when(k > 0)
    def _():
        x3_ref[...] += part

    @pl.when(k == n_k - 1)
    def _():
        hm = _rms_mod(x3_ref[...], gn_ref[...], _mod_part(m_ref, 3), _mod_part(m_ref, 4))
        hm_ref[...] = hm
        logits = jnp.dot(hm, rw_ref[...], preferred_element_type=F32,
                         precision=lax.Precision.HIGHEST)
        lane = lax.broadcasted_iota(I32, logits.shape, 1)
        lg = jnp.where(lane < N_EXPERTS, logits, -jnp.inf)
        m1 = jnp.max(lg, axis=-1, keepdims=True)
        i1 = jnp.min(jnp.where(lg == m1, lane, LANES), axis=-1, keepdims=True)
        lg2 = jnp.where(lane == i1, -jnp.inf, lg)
        m2 = jnp.max(lg2, axis=-1, keepdims=True)
        i2 = jnp.min(jnp.where(lg2 == m2, lane, LANES), axis=-1, keepdims=True)
        e2 = jnp.exp(m2 - m1)
        inv = 1.0 / (1.0 + e2)
        rt_ref[...] = jnp.where(lane == 0, i1.astype(F32),
                                jnp.where(lane == 1, i2.astype(F32),
                                          jnp.where(lane == 2, inv, e2 * inv)))


def _ssd_out(yn, w, x2, mods, gain, rw, row):
    t = yn.shape[0]
    tm, tk = TM_DENSE, 1024
    n_k = SSD_INNER // tk
    return pl.pallas_call(
        functools.partial(_ssd_out_kernel, n_k=n_k),
        grid=(t // tm, n_k),
        in_specs=[pl.BlockSpec((tm, tk), lambda i, k: (i, k)),
                  pl.BlockSpec((tk, D_MODEL), lambda i, k: (k, 0)),
                  pl.BlockSpec((tm, D_MODEL), lambda i, k: (i, 0)),
                  pl.BlockSpec((1, 1, N_MOD * D_MODEL), lambda i, k: (row(i), 0, 0)),
                  pl.BlockSpec((1, D_MODEL), lambda i, k: (0, 0)),
                  pl.BlockSpec((D_MODEL, LANES), lambda i, k: (0, 0))],
        out_specs=[pl.BlockSpec((tm, D_MODEL), lambda i, k: (i, 0)),
                   pl.BlockSpec((tm, D_MODEL), lambda i, k: (i, 0)),
                   pl.BlockSpec((tm, LANES), lambda i, k: (i, 0))],
        out_shape=[jax.ShapeDtypeStruct((t, D_MODEL), F32),
                   jax.ShapeDtypeStruct((t, D_MODEL), F32),
                   jax.ShapeDtypeStruct((t, LANES), F32)],
        compiler_params=_cparams("arbitrary", "arbitrary"),
        name="ssd_out",
    )(yn, w, x2, mods, gain, rw)


def _row_copy(src_hbm, dst_ref, src_row, dst_row, sem):
    return pltpu.make_async_copy(src_hbm.at[pl.ds(src_row, 1), :], dst_ref.at[pl.ds(dst_row, 1), :], sem)


def _gather_kernel(src_ref, nrows_ref, hm_hbm, o_ref, buf_ref, sem):
    j = pl.program_id(0)
    tg = o_ref.shape[0]

    @pl.when(j * tg < nrows_ref[0])
    def _():
        def issue(r, c):
            _row_copy(hm_hbm, buf_ref, src_ref[j * tg + r], r, sem).start()
            return c
        lax.fori_loop(0, tg, issue, 0)
        pltpu.make_async_copy(hm_hbm.at[pl.ds(0, tg), :], buf_ref, sem).wait()
        o_ref[...] = buf_ref[...].astype(BF16)

    @pl.when(j * tg >= nrows_ref[0])
    def _():
        o_ref[...] = jnp.zeros(o_ref.shape, BF16)


def _gather(row_src, nrows, hm, n_rows_padded):
    return pl.pallas_call(
        _gather_kernel,
        grid_spec=pltpu.PrefetchScalarGridSpec(
            num_scalar_prefetch=2, grid=(n_rows_padded // TG,),
            in_specs=[pl.BlockSpec(memory_space=pl.ANY)],
            out_specs=pl.BlockSpec((TG, D_MODEL), lambda j, src, nr: (j, 0)),
            scratch_shapes=[pltpu.VMEM((TG, D_MODEL), F32), pltpu.SemaphoreType.DMA]),
        out_shape=jax.ShapeDtypeStruct((n_rows_padded, D_MODEL), BF16),
        compiler_params=_cparams("arbitrary"),
        name="moe_gather",
    )(row_src, nrows, hm)


def _moe_kernel(te_ref, nv_ref, x_ref, wg_ref, wu_ref, wd_ref, y_ref, *, cn):
    i, f = pl.program_id(0), pl.program_id(1)
    valid = i < nv_ref[0]

    @pl.when(f == 0)
    def _():
        y_ref[...] = jnp.zeros(y_ref.shape, F32)

    @pl.when(valid)
    def _():
        x = x_ref[...]
        a = (_silu(_dot(x, wg_ref[...].astype(BF16))) * _dot(x, wu_ref[...].astype(BF16))).astype(BF16)
        for n in range(D_MODEL // cn):
            y_ref[:, n * cn:(n + 1) * cn] += _dot(a, wd_ref[:, n * cn:(n + 1) * cn].astype(BF16))


def _moe(tile_expert, nvalid, xs, wgu, wd):
    r = xs.shape[0]
    tm, tf = TM_MOE, TF_MOE
    n_f = MOE_DFF // tf

    def fmap(i, f, nv):
        return jnp.where(i < nv[0], f, n_f - 1)

    return pl.pallas_call(
        functools.partial(_moe_kernel, cn=512),
        grid_spec=pltpu.PrefetchScalarGridSpec(
            num_scalar_prefetch=2, grid=(r // tm, n_f),
            in_specs=[pl.BlockSpec((tm, D_MODEL), lambda i, f, te, nv: (jnp.minimum(i, nv[0] - 1), 0)),
                      pl.BlockSpec((None, D_MODEL, tf), lambda i, f, te, nv: (te[i], 0, fmap(i, f, nv))),
                      pl.BlockSpec((None, D_MODEL, tf),
                                   lambda i, f, te, nv: (te[i], 0, n_f + fmap(i, f, nv))),
                      pl.BlockSpec((None, tf, D_MODEL), lambda i, f, te, nv: (te[i], fmap(i, f, nv), 0))],
            out_specs=pl.BlockSpec((tm, D_MODEL), lambda i, f, te, nv: (i, 0))),
        out_shape=jax.ShapeDtypeStruct((r, D_MODEL), F32),
        compiler_params=_cparams("arbitrary", "arbitrary"),
        name="moe_ffn",
    )(tile_expert, nvalid, xs, wgu, wgu, wd)


def _combine_kernel(pos_ref, y_hbm, x3_ref, rt_ref, m_ref, gn_ref, o_ref, ya_ref, yb_ref, sem_a, sem_b):
    i = pl.program_id(0)
    tc = o_ref.shape[0]

    def issue(r, c):
        t2 = 2 * (i * tc + r)
        _row_copy(y_hbm, ya_ref, pos_ref[t2], r, sem_a).start()
        _row_copy(y_hbm, yb_ref, pos_ref[t2 + 1], r, sem_b).start()
        return c
    lax.fori_loop(0, tc, issue, 0)
    pltpu.make_async_copy(y_hbm.at[pl.ds(0, tc), :], ya_ref, sem_a).wait()
    pltpu.make_async_copy(y_hbm.at[pl.ds(0, tc), :], yb_ref, sem_b).wait()
    rt = rt_ref[...]
    moe = rt[:, 2:3] * ya_ref[...] + rt[:, 3:4] * yb_ref[...]
    x4 = x3_ref[...] + _mod_part(m_ref, 5) * moe
    ms = jnp.mean(x4 * x4, axis=-1, keepdims=True)
    o_ref[...] = x4 * lax.rsqrt(ms + EPS) * gn_ref[...]


def _combine(pos, y, x3, route, mods, gain, seq):
    t = x3.shape[0]
    tpb = seq // TC
    return pl.pallas_call(
        _combine_kernel,
        grid_spec=pltpu.PrefetchScalarGridSpec(
            num_scalar_prefetch=1, grid=(t // TC,),
            in_specs=[pl.BlockSpec(memory_space=pl.ANY),
                      pl.BlockSpec((TC, D_MODEL), lambda i, p: (i, 0)),
                      pl.BlockSpec((TC, LANES), lambda i, p: (i, 0)),
                      pl.BlockSpec((1, 1, N_MOD * D_MODEL), lambda i, p: (i // tpb, 0, 0)),
                      pl.BlockSpec((1, D_MODEL), lambda i, p: (0, 0))],
            out_specs=pl.BlockSpec((TC, D_MODEL), lambda i, p: (i, 0)),
            scratch_shapes=[pltpu.VMEM((TC, D_MODEL), F32), pltpu.VMEM((TC, D_MODEL), F32),
                            pltpu.SemaphoreType.DMA, pltpu.SemaphoreType.DMA]),
        out_shape=jax.ShapeDtypeStruct((t, D_MODEL), F32),
        compiler_params=_cparams("arbitrary"),
        name="moe_combine",
    )(pos, y, x3, route, mods, gain)


def _routing_tables(route, n_tok):
    e_flat = route[:, 0:TOP_K].astype(I32).reshape(-1)
    onehot = (e_flat[:, None] == jnp.arange(N_EXPERTS, dtype=I32)[None, :]).astype(I32)
    csum = jnp.cumsum(onehot, axis=0)
    counts = csum[-1]
    rank = jnp.take_along_axis(csum, e_flat[:, None], axis=1)[:, 0] - 1
    tiles = (counts + TM_MOE - 1) // TM_MOE
    tile_end = jnp.cumsum(tiles)
    pos = (tile_end - tiles)[e_flat] * TM_MOE + rank
    n_tiles = n_tok * TOP_K // TM_MOE + N_EXPERTS
    nvalid = tile_end[-1]
    row_src = jnp.zeros((n_tiles * TM_MOE,), I32).at[pos].set(
        jnp.arange(n_tok * TOP_K, dtype=I32) // TOP_K)
    tid = jnp.minimum(jnp.arange(n_tiles, dtype=I32), nvalid - 1)
    tile_expert = jnp.minimum(jnp.searchsorted(tile_end, tid, side="right"), N_EXPERTS - 1).astype(I32)
    return pos.astype(I32), row_src, tile_expert, nvalid.reshape(1).astype(I32)


def _sincos_pos_2d(rows):
    quarter, half = D_MODEL // 4, D_MODEL // 2
    omega = 1.0 / (POS_BASE ** (jnp.arange(quarter, dtype=F32) / quarter))

    def emb1d(n):
        ang = jnp.arange(n, dtype=F32)[:, None] * omega[None, :]
        return jnp.concatenate([jnp.sin(ang), jnp.cos(ang)], axis=-1)

    er, ec = emb1d(rows), emb1d(GRID_W)
    grid = jnp.concatenate([jnp.broadcast_to(er[:, None, :], (rows, GRID_W, half)),
                            jnp.broadcast_to(ec[None, :, :], (rows, GRID_W, half))], axis=-1)
    return grid.reshape(rows * GRID_W, D_MODEL)


def kernel(x, c, ctx, c_ctx, l0_ada_w, l0_ada_b, l0_norm_mix, l0_gmlp_in_w, l0_gmlp_in_b, l0_sgu_ln_g, l0_sgu_ln_b, l0_sgu_ws, l0_sgu_bs, l0_gmlp_out_w, l0_norm_ffn, l0_ffn_w_gu, l0_ffn_w_down, l1_ada_w, l1_ada_b, l1_norm_mix, l1_ssd_in_w, l1_conv_w, l1_conv_b, l1_dt_bias, l1_a_log, l1_d_skip, l1_ssd_norm_g, l1_ssd_out_w, l1_norm_ffn, l1_router_w, l1_moe_w_gu, l1_moe_w_down, final_norm):
    batch, seq, _ = x.shape
    seq_c = ctx.shape[1]
    t_lat, t_ctx = batch * seq, batch * seq_c
    assert batch + 1 <= 8 and seq % TM_WIDE == 0 and t_ctx % TM_WIDE == 0 and seq_c % SSD_CHUNK == 0
    row = _mod_row_map(t_lat // TM_DENSE, seq // TM_DENSE, batch)
    vec = lambda g: g.reshape(1, -1)

    cvec = jnp.concatenate([c, c_ctx[None], jnp.zeros((8 - batch - 1, D_MODEL), F32)], axis=0)
    mods0 = _ada(cvec, l0_ada_w, l0_ada_b)
    mods1 = _ada(cvec, l1_ada_w, l1_ada_b)

    pos = _sincos_pos_2d(seq // GRID_W)
    xr, h = _prep(x.reshape(t_lat, D_MODEL), ctx.reshape(t_ctx, D_MODEL), pos, vec(l0_norm_mix),
                  mods0, batch, seq)
    z, st = _gmlp_in(h, l0_gmlp_in_w, l0_gmlp_in_b)
    x1, h = _sgu_out(z, st, vec(l0_sgu_ln_g), vec(l0_sgu_ln_b), l0_sgu_ws, l0_sgu_bs[:, :, None],
                     l0_gmlp_out_w.astype(BF16), xr, mods0, vec(l0_norm_ffn), row)
    x2, h = _ffn(h, l0_ffn_w_gu.astype(BF16), l0_ffn_w_down.astype(BF16), x1, mods0, mods1,
                 vec(l1_norm_mix), row)

    zx, dt = _ssd_in(h, l1_ssd_in_w, l1_dt_bias)
    xbc_l = _conv(zx, l1_conv_w, l1_conv_b, batch, seq, 0)
    xbc_c = _conv(zx, l1_conv_w, l1_conv_b, batch, seq_c, t_lat)
    yn = _ssd(xbc_l, xbc_c, dt, zx, l1_a_log, l1_d_skip, l1_ssd_norm_g, batch, seq, seq_c)
    rw = jnp.pad(l1_router_w, ((0, 0), (0, LANES - N_EXPERTS)))
    x3, hm, route = _ssd_out(yn, l1_ssd_out_w.astype(BF16), x2, mods1, vec(l1_norm_ffn), rw, row)

    pos_tab, row_src, tile_expert, nvalid = _routing_tables(route, t_lat)
    xs = _gather(row_src, nvalid * TM_MOE, hm, row_src.shape[0])
    ys = _moe(tile_expert, nvalid, xs, l1_moe_w_gu, l1_moe_w_down)
    out = _combine(pos_tab, ys, x3, route, mods1, vec(final_norm), seq)
    return out.reshape(batch, seq, D_MODEL)
```

```python
import functools
import math

import jax
import jax.numpy as jnp
from jax import lax
from jax.experimental import pallas as pl
from jax.experimental.pallas import tpu as pltpu

F32 = jnp.float32
BF16 = jnp.bfloat16
I32 = jnp.int32

D_MODEL = 2048
GRID_W = 64
EPS = 1e-6
POS_BASE = 10000.0
N_MOD = 6
CHUNK = 128
GMLP_DFF = 6 * D_MODEL
GMLP_HALF = GMLP_DFF // 2
GMLP_GROUPS = 8
GMLP_GC = GMLP_HALF // GMLP_GROUPS
SSD_INNER = 2 * D_MODEL
SSD_HEADDIM = 64
SSD_HEADS = SSD_INNER // SSD_HEADDIM
SSD_STATE = 128
SSD_GROUPS = 8
SSD_RPG = SSD_HEADS // SSD_GROUPS
SSD_GW = SSD_RPG * SSD_HEADDIM
SSD_CONV = 5
SSD_CHUNK = 128
SSD_BC = SSD_GROUPS * SSD_STATE
SSD_CONV_CH = SSD_INNER + 2 * SSD_BC
SSD_ZX = SSD_INNER + SSD_CONV_CH
FFN_DFF = 5632
N_EXPERTS = 8
TOP_K = 2
MOE_DFF = 7168

LANES = 128
INV_SQRT2 = 0.7071067811865476
LOG2E = 1.4426950408889634
NEG_BIG = -1e30
VMEM_LIMIT = 56 * 1024 * 1024

TM_DENSE = 512
TM_WIDE = 1024
TM_MOE = 1024
TF_MOE = 256
N_F_MOE = MOE_DFF // TF_MOE
TC = 256
PACK_ROWS = D_MODEL // (2 * LANES)
PF_ROWS = -(-TM_MOE // N_F_MOE)
XBUF_ROWS = PF_ROWS * N_F_MOE


def _cparams(*sem):
    return pltpu.CompilerParams(dimension_semantics=sem, vmem_limit_bytes=VMEM_LIMIT)


def _silu(x):
    return x * (1.0 / (1.0 + jnp.exp(-x)))


def _dot(a, b):
    return jnp.dot(a, b, preferred_element_type=F32)


def _dot_nt(a, b):
    return lax.dot_general(a, b, (((1,), (1,)), ((), ())), preferred_element_type=F32)


def _dot_tn(a, b):
    return lax.dot_general(a, b, (((0,), (0,)), ((), ())), preferred_element_type=F32)


def _rms_mod(x, gain, shift, scale):
    ms = jnp.mean(x * x, axis=-1, keepdims=True)
    y = x * lax.rsqrt(ms + EPS) * gain
    return y * (1.0 + scale) + shift


def _acc_cols(acc_ref, gate, lhs, w_ref, cn=512):
    for n in range(acc_ref.shape[1] // cn):
        cols = slice(n * cn, (n + 1) * cn)
        acc_ref[:, cols] += gate[:, cols] * _dot(lhs, w_ref[:, cols])


def _mod_part(m_ref, k):
    return m_ref[0, :, k * D_MODEL:(k + 1) * D_MODEL]


def _split3(v):
    hi = v.astype(BF16)
    r1 = v - hi.astype(F32)
    mid = r1.astype(BF16)
    lo = (r1 - mid.astype(F32)).astype(BF16)
    return hi, mid, lo


def _ada_kernel(c_ref, w_ref, b_ref, o_ref):
    s = _silu(c_ref[...]).astype(BF16)
    o_ref[...] = _dot(s, w_ref[...].astype(BF16)) + b_ref[...]


def _ada(cvec, w, b):
    n = w.shape[1]
    tn = 1024
    out = pl.pallas_call(
        _ada_kernel,
        grid=(n // tn,),
        in_specs=[pl.BlockSpec((8, D_MODEL), lambda j: (0, 0)),
                  pl.BlockSpec((D_MODEL, tn), lambda j: (0, j)),
                  pl.BlockSpec((1, tn), lambda j: (0, j))],
        out_specs=pl.BlockSpec((8, tn), lambda j: (0, j)),
        out_shape=jax.ShapeDtypeStruct((8, n), F32),
        compiler_params=_cparams("arbitrary"),
        name="ada",
    )(cvec, w, b.reshape(1, n))
    return out.reshape(8, 1, n)


def _prep_kernel(x_ref, c_ref, pos_ref, g_ref, m_ref, xr_ref, h_ref, *, n_lat):
    i = pl.program_id(0)
    sh, sc = _mod_part(m_ref, 0), _mod_part(m_ref, 1)

    def emit(xv):
        xr_ref[...] = xv
        h_ref[...] = _rms_mod(xv, g_ref[...], sh, sc).astype(BF16)

    @pl.when(i < n_lat)
    def _():
        emit(x_ref[...] + pos_ref[...])

    @pl.when(i >= n_lat)
    def _():
        emit(c_ref[...])


def _mod_row_map(n_lat, tiles_per_batch, ctx_row):
    def row(i):
        return jnp.where(i < n_lat, i // tiles_per_batch, ctx_row)
    return row


def _prep(x2, c2, pos, gain, mods, batch, seq):
    tm = TM_DENSE
    n_lat, n_ctx = x2.shape[0] // tm, c2.shape[0] // tm
    tpb = seq // tm
    row = _mod_row_map(n_lat, tpb, batch)
    t = x2.shape[0] + c2.shape[0]
    return pl.pallas_call(
        functools.partial(_prep_kernel, n_lat=n_lat),
        grid=(n_lat + n_ctx,),
        in_specs=[pl.BlockSpec((tm, D_MODEL), lambda i: (jnp.minimum(i, n_lat - 1), 0)),
                  pl.BlockSpec((tm, D_MODEL), lambda i: (jnp.maximum(i - n_lat, 0), 0)),
                  pl.BlockSpec((tm, D_MODEL), lambda i: (i % tpb, 0)),
                  pl.BlockSpec((1, D_MODEL), lambda i: (0, 0)),
                  pl.BlockSpec((1, 1, N_MOD * D_MODEL), lambda i: (row(i), 0, 0))],
        out_specs=[pl.BlockSpec((tm, D_MODEL), lambda i: (i, 0)),
                   pl.BlockSpec((tm, D_MODEL), lambda i: (i, 0))],
        out_shape=[jax.ShapeDtypeStruct((t, D_MODEL), F32),
                   jax.ShapeDtypeStruct((t, D_MODEL), BF16)],
        compiler_params=_cparams("arbitrary"),
        name="prep",
    )(x2, c2, pos, gain, mods)


def _gmlp_in_kernel(h_ref, w_ref, b_ref, z_ref, st_ref, *, n_u, tn, cw):
    j = pl.program_id(1)
    h = h_ref[...]
    tm = h.shape[0]
    s1 = jnp.zeros((tm, LANES), F32)
    s2 = jnp.zeros((tm, LANES), F32)
    for c in range(tn // cw):
        a = _dot(h, w_ref[:, c * cw:(c + 1) * cw].astype(BF16)) + b_ref[:, c * cw:(c + 1) * cw]
        g = 0.5 * a * (1.0 + lax.erf(a * INV_SQRT2))
        z_ref[:, c * cw:(c + 1) * cw] = g.astype(BF16)
        for q in range(cw // LANES):
            gq = g[:, q * LANES:(q + 1) * LANES]
            s1 = s1 + gq
            s2 = s2 + gq * gq

    @pl.when(j == n_u)
    def _():
        st_ref[:, 0:LANES] = s1
        st_ref[:, LANES:2 * LANES] = s2

    @pl.when(j > n_u)
    def _():
        st_ref[:, 0:LANES] += s1
        st_ref[:, LANES:2 * LANES] += s2


def _gmlp_in(h, w, b):
    t = h.shape[0]
    tm, tn = TM_WIDE, 1024
    n = w.shape[1]
    return pl.pallas_call(
        functools.partial(_gmlp_in_kernel, n_u=GMLP_HALF // tn, tn=tn, cw=256),
        grid=(t // tm, n // tn),
        in_specs=[pl.BlockSpec((tm, D_MODEL), lambda i, j: (i, 0)),
                  pl.BlockSpec((D_MODEL, tn), lambda i, j: (0, j)),
                  pl.BlockSpec((1, tn), lambda i, j: (0, j))],
        out_specs=[pl.BlockSpec((tm, tn), lambda i, j: (i, j)),
                   pl.BlockSpec((tm, 2 * LANES), lambda i, j: (i, 0))],
        out_shape=[jax.ShapeDtypeStruct((t, n), BF16),
                   jax.ShapeDtypeStruct((t, 2 * LANES), F32)],
        compiler_params=_cparams("arbitrary", "arbitrary"),
        name="gmlp_in",
    )(h, w, b.reshape(1, n))


def _sgu_out_kernel(zu_ref, zv_ref, st_ref, lg_ref, lb_ref, ws_ref, bs_ref, wo_ref, xr_ref,
                    m_ref, gn_ref, x1_ref, hn_ref, gs_ref, *, n_k):
    k = pl.program_id(1)
    tm = zu_ref.shape[0]
    st = st_ref[...]
    mu = jnp.sum(st[:, 0:LANES], axis=-1, keepdims=True) * (1.0 / GMLP_HALF)
    ex2 = jnp.sum(st[:, LANES:2 * LANES], axis=-1, keepdims=True) * (1.0 / GMLP_HALF)
    rstd = lax.rsqrt(ex2 - mu * mu + EPS)
    ws = ws_ref[0].astype(BF16)
    bs = bs_ref[0]
    for r in range(tm // CHUNK):
        rows = slice(r * CHUNK, (r + 1) * CHUNK)
        v = zv_ref[rows, :].astype(F32)
        vn = (v - mu[rows]) * rstd[rows] * lg_ref[...] + lb_ref[...]
        s = _dot(ws, vn.astype(BF16)) + bs
        gs_ref[rows, :] = (zu_ref[rows, :].astype(F32) * s).astype(BF16)

    @pl.when(k == 0)
    def _():
        x1_ref[...] = xr_ref[...]

    _acc_cols(x1_ref, _mod_part(m_ref, 2), gs_ref[...], wo_ref)

    @pl.when(k == n_k - 1)
    def _():
        hn_ref[...] = _rms_mod(x1_ref[...], gn_ref[...], _mod_part(m_ref, 3),
                               _mod_part(m_ref, 4)).astype(BF16)


def _sgu_out(z, st, lg, lb, ws, bs, wo, xr, mods, gain_next, row):
    t = xr.shape[0]
    tm, n_k = TM_DENSE, GMLP_GROUPS
    return pl.pallas_call(
        functools.partial(_sgu_out_kernel, n_k=n_k),
        grid=(t // tm, n_k),
        in_specs=[pl.BlockSpec((tm, GMLP_GC), lambda i, k: (i, k)),
                  pl.BlockSpec((tm, GMLP_GC), lambda i, k: (i, n_k + k)),
                  pl.BlockSpec((tm, 2 * LANES), lambda i, k: (i, 0)),
                  pl.BlockSpec((1, GMLP_GC), lambda i, k: (0, k)),
                  pl.BlockSpec((1, GMLP_GC), lambda i, k: (0, k)),
                  pl.BlockSpec((1, CHUNK, CHUNK), lambda i, k: (k, 0, 0)),
                  pl.BlockSpec((1, CHUNK, 1), lambda i, k: (k, 0, 0)),
                  pl.BlockSpec((GMLP_GC, D_MODEL), lambda i, k: (k, 0)),
                  pl.BlockSpec((tm, D_MODEL), lambda i, k: (i, 0)),
                  pl.BlockSpec((1, 1, N_MOD * D_MODEL), lambda i, k: (row(i), 0, 0)),
                  pl.BlockSpec((1, D_MODEL), lambda i, k: (0, 0))],
        out_specs=[pl.BlockSpec((tm, D_MODEL), lambda i, k: (i, 0)),
                   pl.BlockSpec((tm, D_MODEL), lambda i, k: (i, 0))],
        out_shape=[jax.ShapeDtypeStruct((t, D_MODEL), F32),
                   jax.ShapeDtypeStruct((t, D_MODEL), BF16)],
        scratch_shapes=[pltpu.VMEM((tm, GMLP_GC), BF16)],
        compiler_params=_cparams("arbitrary", "arbitrary"),
        name="sgu_out",
    )(z, z, st, lg, lb, ws, bs, wo, xr, mods, gain_next)


def _ffn_kernel(h_ref, wg_ref, wu_ref, wd_ref, x1_ref, m_ref, mn_ref, gn_ref, x2_ref, hn_ref, *, n_f):
    f = pl.program_id(1)
    h = h_ref[...]

    @pl.when(f == 0)
    def _():
        x2_ref[...] = x1_ref[...]

    a = (_silu(_dot(h, wg_ref[...])) * _dot(h, wu_ref[...])).astype(BF16)
    _acc_cols(x2_ref, _mod_part(m_ref, 5), a, wd_ref)

    @pl.when(f == n_f - 1)
    def _():
        hn_ref[...] = _rms_mod(x2_ref[...], gn_ref[...], _mod_part(mn_ref, 0),
                               _mod_part(mn_ref, 1)).astype(BF16)


def _ffn(h, wgu, wd, x1, mods, mods_next, gain_next, row):
    t = h.shape[0]
    tm, tf = TM_DENSE, 512
    n_f = FFN_DFF // tf
    mspec = pl.BlockSpec((1, 1, N_MOD * D_MODEL), lambda i, f: (row(i), 0, 0))
    return pl.pallas_call(
        functools.partial(_ffn_kernel, n_f=n_f),
        grid=(t // tm, n_f),
        in_specs=[pl.BlockSpec((tm, D_MODEL), lambda i, f: (i, 0)),
                  pl.BlockSpec((D_MODEL, tf), lambda i, f: (0, f)),
                  pl.BlockSpec((D_MODEL, tf), lambda i, f: (0, n_f + f)),
                  pl.BlockSpec((tf, D_MODEL), lambda i, f: (f, 0)),
                  pl.BlockSpec((tm, D_MODEL), lambda i, f: (i, 0)),
                  mspec, mspec,
                  pl.BlockSpec((1, D_MODEL), lambda i, f: (0, 0))],
        out_specs=[pl.BlockSpec((tm, D_MODEL), lambda i, f: (i, 0)),
                   pl.BlockSpec((tm, D_MODEL), lambda i, f: (i, 0))],
        out_shape=[jax.ShapeDtypeStruct((t, D_MODEL), F32),
                   jax.ShapeDtypeStruct((t, D_MODEL), BF16)],
        compiler_params=_cparams("arbitrary", "arbitrary"),
        name="ffn",
    )(h, wgu, wgu, wd, x1, mods, mods_next, gain_next)


def _proj_kernel(h_ref, w_ref, o_ref, *, tn, cw):
    h = h_ref[...]
    for c in range(tn // cw):
        o_ref[:, c * cw:(c + 1) * cw] = _dot(
            h, w_ref[:, c * cw:(c + 1) * cw].astype(BF16)).astype(o_ref.dtype)


def _dt_kernel(h_ref, w_ref, b_ref, o_ref):
    a = _dot(h_ref[...], w_ref[...].astype(BF16)) + b_ref[...]
    o_ref[...] = jnp.maximum(a, 0.0) + jnp.log(1.0 + jnp.exp(-jnp.abs(a)))


def _ssd_in(h, w, dt_bias):
    t = h.shape[0]
    tm, tn = TM_WIDE, 1024
    zx = pl.pallas_call(
        functools.partial(_proj_kernel, tn=tn, cw=256),
        grid=(t // tm, SSD_ZX // tn),
        in_specs=[pl.BlockSpec((tm, D_MODEL), lambda i, j: (i, 0)),
                  pl.BlockSpec((D_MODEL, tn), lambda i, j: (0, j))],
        out_specs=pl.BlockSpec((tm, tn), lambda i, j: (i, j)),
        out_shape=jax.ShapeDtypeStruct((t, SSD_ZX), BF16),
        compiler_params=_cparams("arbitrary", "arbitrary"),
        name="ssd_in",
    )(h, w)
    nd = 2 * SSD_HEADS
    dt = pl.pallas_call(
        _dt_kernel,
        grid=(t // tm,),
        in_specs=[pl.BlockSpec((tm, D_MODEL), lambda i: (i, 0)),
                  pl.BlockSpec((D_MODEL, nd), lambda i: (0, SSD_ZX // nd)),
                  pl.BlockSpec((1, nd), lambda i: (0, 0))],
        out_specs=pl.BlockSpec((tm, nd), lambda i: (i, 0)),
        out_shape=jax.ShapeDtypeStruct((t, nd), F32),
        compiler_params=_cparams("arbitrary"),
        name="ssd_dt",
    )(h, w, dt_bias.reshape(1, nd))
    return zx, dt


def _conv_kernel(z_ref, w_ref, b_ref, o_ref, pad_ref, *, seq):
    tc = z_ref.shape[1]
    halo = 8
    pad_ref[0:halo, :] = jnp.zeros((halo, tc), F32)
    pad_ref[halo:halo + seq, :] = z_ref[...].astype(F32)
    pad_ref[halo + seq:2 * halo + seq, :] = jnp.zeros((halo, tc), F32)
    acc = jnp.zeros((seq, tc), F32) + b_ref[...]
    for k in range(SSD_CONV):
        off = halo - SSD_CONV // 2 + k
        acc = acc + pad_ref[off:off + seq, :] * w_ref[k:k + 1, :]
    o_ref[...] = _silu(acc).astype(BF16)


def _conv(zx, w, b, batch, seq, row0):
    tc = 256
    nblk = SSD_CONV_CH // tc
    off_r, off_c = row0 // seq, SSD_INNER // tc
    return pl.pallas_call(
        functools.partial(_conv_kernel, seq=seq),
        grid=(batch, nblk),
        in_specs=[pl.BlockSpec((seq, tc), lambda bi, c: (off_r + bi, off_c + c)),
                  pl.BlockSpec((SSD_CONV, tc), lambda bi, c: (0, c)),
                  pl.BlockSpec((1, tc), lambda bi, c: (0, c))],
        out_specs=pl.BlockSpec((seq, tc), lambda bi, c: (bi, c)),
        out_shape=jax.ShapeDtypeStruct((batch * seq, SSD_CONV_CH), BF16),
        scratch_shapes=[pltpu.VMEM((seq + 16, tc), F32)],
        compiler_params=_cparams("arbitrary", "arbitrary"),
        name="ssd_conv",
    )(zx, w, b.reshape(1, SSD_CONV_CH))


def _cumsum_rows(a, ltri3_ref):
    hi, mid, lo = _split3(a)
    return _dot(ltri3_ref[...], jnp.concatenate([hi, mid, lo], axis=0))


def _expand_heads(v, r3_ref, off):
    pieces = [p.astype(F32)[:, off:off + SSD_HEADS] for p in _split3(v)]
    pieces.append(jnp.zeros_like(pieces[0]))
    return _dot(jnp.concatenate(pieces, axis=1).astype(BF16), r3_ref[...])


def _state_update(s_ref, x_f32, bmat, w1x, decay_row):
    xw = (x_f32 * w1x).astype(BF16)
    for g in range(SSD_GROUPS):
        gs = slice(g * SSD_GW, (g + 1) * SSD_GW)
        bg = bmat[:, g * SSD_STATE:(g + 1) * SSD_STATE]
        s_ref[g] = s_ref[g] * decay_row[:, gs] + _dot_tn(bg, xw[:, gs])


def _state_readout(s_ref, cmat, scale):
    outs = []
    for g in range(SSD_GROUPS):
        gs = slice(g * SSD_GW, (g + 1) * SSD_GW)
        cg = cmat[:, g * SSD_STATE:(g + 1) * SSD_STATE]
        outs.append(_dot(cg, s_ref[g].astype(BF16)) * scale[:, gs])
    return outs


def _ssd_fwd_kernel(xl_ref, bl_ref, cl_ref, xc_ref, bc_ref, dt_ref, alog_ref, dsk_ref, ltri_ref,
                    r3_ref, y_ref, s_ref, *, n_ctx):
    s = pl.program_id(1)
    q = SSD_CHUNK

    @pl.when(s == 0)
    def _():
        s_ref[...] = jnp.zeros(s_ref.shape, F32)

    dt = dt_ref[...]
    a = dt * (-jnp.exp(alog_ref[...]) * LOG2E)
    cs = _cumsum_rows(a, ltri_ref)
    cs_end = cs[q - 1:q, :]
    w1 = dt * jnp.exp2(cs_end - cs)
    ex = _expand_heads(jnp.concatenate([jnp.exp2(cs), w1], axis=0), r3_ref, 0)
    ef, w1x = ex[0:q], ex[q:2 * q]

    @pl.when(s < n_ctx)
    def _():
        _state_update(s_ref, xc_ref[...].astype(F32), bc_ref[...], w1x, ef[q - 1:q])

    @pl.when(s >= n_ctx)
    def _():
        ce = cs - a
        ii = lax.broadcasted_iota(I32, (q, q), 0)
        jj = lax.broadcasted_iota(I32, (q, q), 1)
        lower, upper = ii >= jj, ii <= jj
        lane = lax.broadcasted_iota(I32, (q, LANES), 1)
        ldt = jnp.log2(dt)
        rows = jnp.where(lane < SSD_HEADS, cs - ldt, ce + ldt).T
        x = xl_ref[...]
        bmat, cmat = bl_ref[...], cl_ref[...]
        xf = x.astype(F32)
        yoff = _state_readout(s_ref, cmat, ef)
        nh = SSD_HEADS
        for g in range(SSD_GROUPS):
            gmat = _dot_nt(cmat[:, g * SSD_STATE:(g + 1) * SSD_STATE],
                           bmat[:, g * SSD_STATE:(g + 1) * SSD_STATE])
            for pr in range(SSD_RPG // 2):
                col = g * SSD_GW + pr * LANES
                xp = x[:, col:col + LANES]
                prod = []
                for half in range(2):
                    h = g * SSD_RPG + pr * 2 + half
                    mf = jnp.exp2(jnp.where(lower, cs[:, h:h + 1] - rows[h:h + 1, :], NEG_BIG))
                    mb = jnp.exp2(jnp.where(upper, rows[nh + h:nh + h + 1, :] - ce[:, nh + h:nh + h + 1],
                                            NEG_BIG))
                    prod.append(_dot(((mf + mb) * gmat).astype(BF16), xp))
                y_ref[:, col:col + LANES] = (
                    jnp.where(lane < SSD_HEADDIM, prod[0], prod[1]) + yoff[g][:, pr * LANES:(pr + 1) * LANES]
                    + dsk_ref[:, col:col + LANES] * xf[:, col:col + LANES])
        _state_update(s_ref, xf, bmat, w1x, ef[q - 1:q])


def _ssd_bwd_kernel(xl_ref, bl_ref, cl_ref, xc_ref, bc_ref, dt_ref, alog_ref, ltri_ref, r3_ref,
                    y1_ref, z_ref, ng_ref, o_ref, s_ref, *, n_ctx):
    s = pl.program_id(1)
    q = SSD_CHUNK

    @pl.when(s == 0)
    def _():
        s_ref[...] = jnp.zeros(s_ref.shape, F32)

    dt = dt_ref[...]
    a = dt * (-jnp.exp(alog_ref[...]) * LOG2E)
    cs = _cumsum_rows(a, ltri_ref)
    ce = cs - a
    cs_end = cs[q - 1:q, :]
    w1 = dt * jnp.exp2(ce)
    ex = _expand_heads(jnp.concatenate([jnp.exp2(cs_end - ce), w1], axis=0), r3_ref, SSD_HEADS)
    eb, w1x = ex[0:q], ex[q:2 * q]

    @pl.when(s < n_ctx)
    def _():
        _state_update(s_ref, xc_ref[...].astype(F32), bc_ref[...], w1x, eb[0:1])

    @pl.when(s >= n_ctx)
    def _():
        x = xl_ref[...]
        yoff = _state_readout(s_ref, cl_ref[...], eb)
        for g in range(SSD_GROUPS):
            gs = slice(g * SSD_GW, (g + 1) * SSD_GW)
            yf = (y1_ref[:, gs] + yoff[g]) * _silu(z_ref[:, gs].astype(F32))
            ms = jnp.mean(yf * yf, axis=-1, keepdims=True)
            o_ref[:, gs] = (yf * lax.rsqrt(ms + EPS) * ng_ref[:, gs]).astype(BF16)
        _state_update(s_ref, x.astype(F32), bl_ref[...], w1x, eb[0:1])


def _ssd(xbc_l, xbc_c, dt, zx, a_log, d_skip, norm_g, batch, seq, seq_c):
    q = SSD_CHUNK
    n_lat, n_ctx = seq // q, seq_c // q
    steps = n_ctx + n_lat
    t_lat = batch * seq
    nd = 2 * SSD_HEADS
    cb = SSD_INNER // SSD_BC
    tri = (jnp.arange(q)[:, None] >= jnp.arange(q)[None, :]).astype(BF16)
    ltri3 = jnp.concatenate([tri, tri, tri], axis=1)
    chan_head = jnp.arange(SSD_INNER) // SSD_HEADDIM
    r1 = (jnp.arange(SSD_HEADS)[:, None] == chan_head[None, :]).astype(BF16)
    r3 = jnp.concatenate([r1, r1, r1, jnp.zeros_like(r1)], axis=0)
    alog = a_log.reshape(1, nd)
    dsk = jnp.repeat(d_skip, SSD_HEADDIM).reshape(1, SSD_INNER)

    def lat_f(s):
        return jnp.maximum(s - n_ctx, 0)

    def ctx_f(s):
        return jnp.minimum(s, n_ctx - 1)

    def lat_b(s):
        return n_lat - 1 - jnp.maximum(s - n_ctx, 0)

    def ctx_b(s):
        return n_ctx - 1 - jnp.minimum(s, n_ctx - 1)

    def specs(lat, ctx):
        def dt_row(b, s):
            return jnp.where(s < n_ctx, t_lat // q + b * n_ctx + ctx(s), b * n_lat + lat(s))
        return [pl.BlockSpec((q, SSD_INNER), lambda b, s: (b * n_lat + lat(s), 0)),
                pl.BlockSpec((q, SSD_BC), lambda b, s: (b * n_lat + lat(s), cb)),
                pl.BlockSpec((q, SSD_BC), lambda b, s: (b * n_lat + lat(s), cb + 1)),
                pl.BlockSpec((q, SSD_INNER), lambda b, s: (b * n_ctx + ctx(s), 0)),
                pl.BlockSpec((q, SSD_BC), lambda b, s: (b * n_ctx + ctx(s), cb)),
                pl.BlockSpec((q, nd), lambda b, s: (dt_row(b, s), 0)),
                pl.BlockSpec((1, nd), lambda b, s: (0, 0))]

    const = lambda shape: pl.BlockSpec(shape, lambda b, s: (0,) * len(shape))
    state = pltpu.VMEM((SSD_GROUPS, SSD_STATE, SSD_GW), F32)

    y1 = pl.pallas_call(
        functools.partial(_ssd_fwd_kernel, n_ctx=n_ctx),
        grid=(batch, steps),
        in_specs=specs(lat_f, ctx_f) + [const((1, SSD_INNER)), const((q, 3 * q)),
                                        const((4 * SSD_HEADS, SSD_INNER))],
        out_specs=pl.BlockSpec((q, SSD_INNER), lambda b, s: (b * n_lat + lat_f(s), 0)),
        out_shape=jax.ShapeDtypeStruct((t_lat, SSD_INNER), F32),
        scratch_shapes=[state],
        compiler_params=_cparams("arbitrary", "arbitrary"),
        name="ssd_fwd",
    )(xbc_l, xbc_l, xbc_l, xbc_c, xbc_c, dt, alog, dsk, ltri3, r3)

    yn = pl.pallas_call(
        functools.partial(_ssd_bwd_kernel, n_ctx=n_ctx),
        grid=(batch, steps),
        in_specs=specs(lat_b, ctx_b) + [
            const((q, 3 * q)), const((4 * SSD_HEADS, SSD_INNER)),
            pl.BlockSpec((q, SSD_INNER), lambda b, s: (b * n_lat + lat_b(s), 0)),
            pl.BlockSpec((q, SSD_INNER), lambda b, s: (b * n_lat + lat_b(s), 0)),
            const((1, SSD_INNER))],
        out_specs=pl.BlockSpec((q, SSD_INNER), lambda b, s: (b * n_lat + lat_b(s), 0)),
        out_shape=jax.ShapeDtypeStruct((t_lat, SSD_INNER), BF16),
        scratch_shapes=[state],
        compiler_params=_cparams("arbitrary", "arbitrary"),
        name="ssd_bwd",
    )(xbc_l, xbc_l, xbc_l, xbc_c, xbc_c, dt, alog, ltri3, r3, y1, zx,
      norm_g.reshape(1, SSD_INNER))
    return yn


def _ssd_out_kernel(y_ref, w_ref, x2_ref, m_ref, gn_ref, rw_ref, x3_ref, hp_ref, rt_ref, *, n_k):
    k = pl.program_id(1)
    tm = x3_ref.shape[0]

    @pl.when(k == 0)
    def _():
        x3_ref[...] = x2_ref[...]

    _acc_cols(x3_ref, _mod_part(m_ref, 2), y_ref[...], w_ref)

    @pl.when(k == n_k - 1)
    def _():
        hm = _rms_mod(x3_ref[...], gn_ref[...], _mod_part(m_ref, 3), _mod_part(m_ref, 4))
        bits = pltpu.bitcast(hm.astype(BF16).astype(F32), jnp.uint32)
        for s in range(PACK_ROWS):
            lo = bits[:, s * LANES:(s + 1) * LANES] >> 16
            hi = bits[:, D_MODEL // 2 + s * LANES:D_MODEL // 2 + (s + 1) * LANES]
            hp_ref[pl.ds(s, tm, stride=PACK_ROWS), :] = lo | hi
        logits = jnp.dot(hm, rw_ref[...], preferred_element_type=F32,
                         precision=lax.Precision.HIGHEST)
        lane = lax.broadcasted_iota(I32, logits.shape, 1)
        lg = jnp.where(lane < N_EXPERTS, logits, -jnp.inf)
        m1 = jnp.max(lg, axis=-1, keepdims=True)
        i1 = jnp.min(jnp.where(lg == m1, lane, LANES), axis=-1, keepdims=True)
        lg2 = jnp.where(lane == i1, -jnp.inf, lg)
        m2 = jnp.max(lg2, axis=-1, keepdims=True)
        i2 = jnp.min(jnp.where(lg2 == m2, lane, LANES), axis=-1, keepdims=True)
        e2 = jnp.exp(m2 - m1)
        inv = 1.0 / (1.0 + e2)
        rt_ref[...] = jnp.where(lane == 0, i1.astype(F32),
                                jnp.where(lane == 1, i2.astype(F32),
                                          jnp.where(lane == 2, inv, e2 * inv)))


def _ssd_out(yn, w, x2, mods, gain, rw, row):
    t = yn.shape[0]
    tm, tk = TM_DENSE, 1024
    n_k = SSD_INNER // tk
    return pl.pallas_call(
        functools.partial(_ssd_out_kernel, n_k=n_k),
        grid=(t // tm, n_k),
        in_specs=[pl.BlockSpec((tm, tk), lambda i, k: (i, k)),
                  pl.BlockSpec((tk, D_MODEL), lambda i, k: (k, 0)),
                  pl.BlockSpec((tm, D_MODEL), lambda i, k: (i, 0)),
                  pl.BlockSpec((1, 1, N_MOD * D_MODEL), lambda i, k: (row(i), 0, 0)),
                  pl.BlockSpec((1, D_MODEL), lambda i, k: (0, 0)),
                  pl.BlockSpec((D_MODEL, LANES), lambda i, k: (0, 0))],
        out_specs=[pl.BlockSpec((tm, D_MODEL), lambda i, k: (i, 0)),
                   pl.BlockSpec((tm * PACK_ROWS, LANES), lambda i, k: (i, 0)),
                   pl.BlockSpec((tm, LANES), lambda i, k: (i, 0))],
        out_shape=[jax.ShapeDtypeStruct((t, D_MODEL), F32),
                   jax.ShapeDtypeStruct((t * PACK_ROWS, LANES), jnp.uint32),
                   jax.ShapeDtypeStruct((t, LANES), F32)],
        compiler_params=_cparams("arbitrary", "arbitrary"),
        name="ssd_out",
    )(yn, w, x2, mods, gain, rw)


def _row_copy(src_hbm, dst_ref, src_row, dst_row, sem):
    return pltpu.make_async_copy(src_hbm.at[pl.ds(src_row, 1), :], dst_ref.at[pl.ds(dst_row, 1), :], sem)


def _moe_kernel(src_ref, te_ref, nr_ref, nv_ref, hp_hbm, wg_ref, wu_ref, wd_ref, y_ref, xbuf_ref, x_ref,
                sem, *, n_tiles, n_f, cn):
    i, f = pl.program_id(0), pl.program_id(1)
    slot = i % 2
    nrows = nr_ref[i]
    half = TM_MOE // 2

    def row_copy(tile, r, slot_):
        src = pl.multiple_of(src_ref[tile * TM_MOE + r] * PACK_ROWS, PACK_ROWS)
        dst = pl.multiple_of(r * PACK_ROWS, PACK_ROWS)
        return pltpu.make_async_copy(hp_hbm.at[pl.ds(src, PACK_ROWS), :],
                                     xbuf_ref.at[slot_, pl.ds(dst, PACK_ROWS), :], sem.at[slot_])

    def wait_slot(slot_):
        pltpu.make_async_copy(hp_hbm.at[pl.ds(0, XBUF_ROWS * PACK_ROWS), :], xbuf_ref.at[slot_],
                              sem.at[slot_]).wait()

    @pl.when(jnp.logical_and(i == 0, f == 0))
    def _():
        def body(r, c):
            row_copy(0, r, 0).start()
            return c
        lax.fori_loop(0, XBUF_ROWS, body, 0)

    @pl.when(f == 0)
    def _():
        wait_slot(slot)
        xb = xbuf_ref.at[slot]
        for s in range(PACK_ROWS):
            w = xb[pl.ds(s, TM_MOE, stride=PACK_ROWS), :]
            x_ref[:, s * LANES:(s + 1) * LANES] = pltpu.bitcast(w << 16, F32).astype(BF16)
            x_ref[:, D_MODEL // 2 + s * LANES:D_MODEL // 2 + (s + 1) * LANES] = pltpu.bitcast(
                w & jnp.uint32(0xFFFF0000), F32).astype(BF16)
        y_ref[...] = jnp.zeros(y_ref.shape, F32)

    def prefetch_next():
        nxt = jnp.minimum(i + 1, n_tiles - 1)
        for k in range(PF_ROWS):
            row_copy(nxt, f * PF_ROWS + k, 1 - slot).start()

    def compute(rows):
        prefetch_next()
        x = x_ref[0:rows, :]
        a = (_silu(_dot(x, wg_ref[...].astype(BF16))) * _dot(x, wu_ref[...].astype(BF16))).astype(BF16)
        for n in range(D_MODEL // cn):
            y_ref[0:rows, n * cn:(n + 1) * cn] += _dot(a, wd_ref[:, n * cn:(n + 1) * cn].astype(BF16))

    @pl.when(nrows > half)
    def _():
        compute(TM_MOE)

    @pl.when(jnp.logical_and(nrows > 0, nrows <= half))
    def _():
        compute(half)

    @pl.when(nrows == 0)
    def _():
        prefetch_next()

    @pl.when(jnp.logical_and(i == n_tiles - 1, f == n_f - 1))
    def _():
        wait_slot(1 - slot)


def _moe(row_src, tile_expert, tile_rows, nvalid, hp, wgu, wd):
    n_tiles = tile_expert.shape[0]
    tm, tf, n_f = TM_MOE, TF_MOE, N_F_MOE

    def fmap(i, f, nv):
        return jnp.where(i < nv[0], f, n_f - 1)

    return pl.pallas_call(
        functools.partial(_moe_kernel, n_tiles=n_tiles, n_f=n_f, cn=512),
        grid_spec=pltpu.PrefetchScalarGridSpec(
            num_scalar_prefetch=4, grid=(n_tiles, n_f),
            in_specs=[pl.BlockSpec(memory_space=pl.ANY),
                      pl.BlockSpec((None, D_MODEL, tf),
                                   lambda i, f, src, te, nr, nv: (te[i], 0, fmap(i, f, nv))),
                      pl.BlockSpec((None, D_MODEL, tf),
                                   lambda i, f, src, te, nr, nv: (te[i], 0, n_f + fmap(i, f, nv))),
                      pl.BlockSpec((None, tf, D_MODEL),
                                   lambda i, f, src, te, nr, nv: (te[i], fmap(i, f, nv), 0))],
            out_specs=pl.BlockSpec((tm, D_MODEL), lambda i, f, src, te, nr, nv: (i, 0)),
            scratch_shapes=[pltpu.VMEM((2, XBUF_ROWS * PACK_ROWS, LANES), jnp.uint32),
                            pltpu.VMEM((tm, D_MODEL), BF16),
                            pltpu.SemaphoreType.DMA((2,))]),
        out_shape=jax.ShapeDtypeStruct((n_tiles * tm, D_MODEL), F32),
        compiler_params=_cparams("arbitrary", "arbitrary"),
        name="moe_ffn",
    )(row_src, tile_expert, tile_rows, nvalid, hp, wgu, wgu, wd)


def _combine_kernel(pos_ref, y_hbm, x3_ref, rt_ref, m_ref, gn_ref, o_ref, ya_ref, yb_ref, sem_a, sem_b):
    i = pl.program_id(0)
    tc = o_ref.shape[0]

    def issue(r, c):
        t2 = 2 * (i * tc + r)
        _row_copy(y_hbm, ya_ref, pos_ref[t2], r, sem_a).start()
        _row_copy(y_hbm, yb_ref, pos_ref[t2 + 1], r, sem_b).start()
        return c
    lax.fori_loop(0, tc, issue, 0, unroll=8)
    pltpu.make_async_copy(y_hbm.at[pl.ds(0, tc), :], ya_ref, sem_a).wait()
    pltpu.make_async_copy(y_hbm.at[pl.ds(0, tc), :], yb_ref, sem_b).wait()
    rt = rt_ref[...]
    moe = rt[:, 2:3] * ya_ref[...] + rt[:, 3:4] * yb_ref[...]
    x4 = x3_ref[...] + _mod_part(m_ref, 5) * moe
    ms = jnp.mean(x4 * x4, axis=-1, keepdims=True)
    o_ref[...] = x4 * lax.rsqrt(ms + EPS) * gn_ref[...]


def _combine(pos, y, x3, route, mods, gain, seq):
    t = x3.shape[0]
    tpb = seq // TC
    return pl.pallas_call(
        _combine_kernel,
        grid_spec=pltpu.PrefetchScalarGridSpec(
            num_scalar_prefetch=1, grid=(t // TC,),
            in_specs=[pl.BlockSpec(memory_space=pl.ANY),
                      pl.BlockSpec((TC, D_MODEL), lambda i, p: (i, 0)),
                      pl.BlockSpec((TC, LANES), lambda i, p: (i, 0)),
                      pl.BlockSpec((1, 1, N_MOD * D_MODEL), lambda i, p: (i // tpb, 0, 0)),
                      pl.BlockSpec((1, D_MODEL), lambda i, p: (0, 0))],
            out_specs=pl.BlockSpec((TC, D_MODEL), lambda i, p: (i, 0)),
            scratch_shapes=[pltpu.VMEM((TC, D_MODEL), F32), pltpu.VMEM((TC, D_MODEL), F32),
                            pltpu.SemaphoreType.DMA, pltpu.SemaphoreType.DMA]),
        out_shape=jax.ShapeDtypeStruct((t, D_MODEL), F32),
        compiler_params=_cparams("arbitrary"),
        name="moe_combine",
    )(pos, y, x3, route, mods, gain)


def _routing_tables(route, n_tok):
    e_flat = route[:, 0:TOP_K].astype(I32).reshape(-1)
    onehot = (e_flat[:, None] == jnp.arange(N_EXPERTS, dtype=I32)[None, :]).astype(I32)
    csum = jnp.cumsum(onehot, axis=0)
    counts = csum[-1]
    rank = jnp.take_along_axis(csum, e_flat[:, None], axis=1)[:, 0] - 1
    tiles = (counts + TM_MOE - 1) // TM_MOE
    tile_end = jnp.cumsum(tiles)
    pos = (tile_end - tiles)[e_flat] * TM_MOE + rank
    n_tiles = n_tok * TOP_K // TM_MOE + N_EXPERTS
    nvalid = tile_end[-1]
    row_src = jnp.zeros((n_tiles * TM_MOE + XBUF_ROWS - TM_MOE,), I32).at[pos].set(
        jnp.arange(n_tok * TOP_K, dtype=I32) // TOP_K)
    tid = jnp.arange(n_tiles, dtype=I32)
    tile_expert = jnp.minimum(jnp.searchsorted(tile_end, jnp.minimum(tid, nvalid - 1), side="right"),
                              N_EXPERTS - 1).astype(I32)
    local = tid - (tile_end - tiles)[tile_expert]
    tile_rows = jnp.where(tid < nvalid, jnp.clip(counts[tile_expert] - local * TM_MOE, 0, TM_MOE), 0)
    return pos.astype(I32), row_src, tile_expert, tile_rows.astype(I32), nvalid.reshape(1).astype(I32)


def _sincos_pos_2d(rows):
    quarter, half = D_MODEL // 4, D_MODEL // 2
    omega = 1.0 / (POS_BASE ** (jnp.arange(quarter, dtype=F32) / quarter))

    def emb1d(n):
        ang = jnp.arange(n, dtype=F32)[:, None] * omega[None, :]
        return jnp.concatenate([jnp.sin(ang), jnp.cos(ang)], axis=-1)

    er, ec = emb1d(rows), emb1d(GRID_W)
    grid = jnp.concatenate([jnp.broadcast_to(er[:, None, :], (rows, GRID_W, half)),
                            jnp.broadcast_to(ec[None, :, :], (rows, GRID_W, half))], axis=-1)
    return grid.reshape(rows * GRID_W, D_MODEL)


def kernel(x, c, ctx, c_ctx, l0_ada_w, l0_ada_b, l0_norm_mix, l0_gmlp_in_w, l0_gmlp_in_b, l0_sgu_ln_g, l0_sgu_ln_b, l0_sgu_ws, l0_sgu_bs, l0_gmlp_out_w, l0_norm_ffn, l0_ffn_w_gu, l0_ffn_w_down, l1_ada_w, l1_ada_b, l1_norm_mix, l1_ssd_in_w, l1_conv_w, l1_conv_b, l1_dt_bias, l1_a_log, l1_d_skip, l1_ssd_norm_g, l1_ssd_out_w, l1_norm_ffn, l1_router_w, l1_moe_w_gu, l1_moe_w_down, final_norm):
    batch, seq, _ = x.shape
    seq_c = ctx.shape[1]
    t_lat, t_ctx = batch * seq, batch * seq_c
    assert batch + 1 <= 8 and seq % TM_WIDE == 0 and t_ctx % TM_WIDE == 0 and seq_c % SSD_CHUNK == 0
    row = _mod_row_map(t_lat // TM_DENSE, seq // TM_DENSE, batch)
    vec = lambda g: g.reshape(1, -1)

    cvec = jnp.concatenate([c, c_ctx[None], jnp.zeros((8 - batch - 1, D_MODEL), F32)], axis=0)
    mods0 = _ada(cvec, l0_ada_w, l0_ada_b)
    mods1 = _ada(cvec, l1_ada_w, l1_ada_b)

    pos = _sincos_pos_2d(seq // GRID_W)
    xr, h = _prep(x.reshape(t_lat, D_MODEL), ctx.reshape(t_ctx, D_MODEL), pos, vec(l0_norm_mix),
                  mods0, batch, seq)
    z, st = _gmlp_in(h, l0_gmlp_in_w, l0_gmlp_in_b)
    x1, h = _sgu_out(z, st, vec(l0_sgu_ln_g), vec(l0_sgu_ln_b), l0_sgu_ws, l0_sgu_bs[:, :, None],
                     l0_gmlp_out_w.astype(BF16), xr, mods0, vec(l0_norm_ffn), row)
    x2, h = _ffn(h, l0_ffn_w_gu.astype(BF16), l0_ffn_w_down.astype(BF16), x1, mods0, mods1,
                 vec(l1_norm_mix), row)

    zx, dt = _ssd_in(h, l1_ssd_in_w, l1_dt_bias)
    xbc_l = _conv(zx, l1_conv_w, l1_conv_b, batch, seq, 0)
    xbc_c = _conv(zx, l1_conv_w, l1_conv_b, batch, seq_c, t_lat)
    yn = _ssd(xbc_l, xbc_c, dt, zx, l1_a_log, l1_d_skip, l1_ssd_norm_g, batch, seq, seq_c)
    rw = jnp.pad(l1_router_w, ((0, 0), (0, LANES - N_EXPERTS)))
    x3, hp, route = _ssd_out(yn, l1_ssd_out_w.astype(BF16), x2, mods1, vec(l1_norm_ffn), rw, row)

    pos_tab, row_src, tile_expert, tile_rows, nvalid = _routing_tables(route, t_lat)
    ys = _moe(row_src, tile_expert, tile_rows, nvalid, hp, l1_moe_w_gu, l1_moe_w_down)
    out = _combine(pos_tab, ys, x3, route, mods1, vec(final_norm), seq)
    return out.reshape(batch, seq, D_MODEL)
```

```python
import functools
import math

import jax
import jax.numpy as jnp
from jax import lax
from jax.experimental import pallas as pl
from jax.experimental.pallas import tpu as pltpu

F32 = jnp.float32
BF16 = jnp.bfloat16
I32 = jnp.int32

D_MODEL = 2048
GRID_W = 64
EPS = 1e-6
POS_BASE = 10000.0
N_MOD = 6
CHUNK = 128
GMLP_DFF = 6 * D_MODEL
GMLP_HALF = GMLP_DFF // 2
GMLP_GROUPS = 8
GMLP_GC = GMLP_HALF // GMLP_GROUPS
SSD_INNER = 2 * D_MODEL
SSD_HEADDIM = 64
SSD_HEADS = SSD_INNER // SSD_HEADDIM
SSD_STATE = 128
SSD_GROUPS = 8
SSD_RPG = SSD_HEADS // SSD_GROUPS
SSD_GW = SSD_RPG * SSD_HEADDIM
SSD_CONV = 5
SSD_CHUNK = 128
SSD_BC = SSD_GROUPS * SSD_STATE
SSD_CONV_CH = SSD_INNER + 2 * SSD_BC
SSD_ZX = SSD_INNER + SSD_CONV_CH
FFN_DFF = 5632
N_EXPERTS = 8
TOP_K = 2
MOE_DFF = 7168

LANES = 128
INV_SQRT2 = 0.7071067811865476
LOG2E = 1.4426950408889634
NEG_BIG = -1e30
VMEM_LIMIT = 56 * 1024 * 1024

TM_DENSE = 512
TM_WIDE = 1024
TM_MOE = 1024
TF_MOE = 256
N_F_MOE = MOE_DFF // TF_MOE
TC = 256
PACK_ROWS = D_MODEL // (2 * LANES)
PF_ROWS = -(-TM_MOE // N_F_MOE)
XBUF_ROWS = PF_ROWS * N_F_MOE


def _cparams(*sem):
    return pltpu.CompilerParams(dimension_semantics=sem, vmem_limit_bytes=VMEM_LIMIT)


def _silu(x):
    return x * (1.0 / (1.0 + jnp.exp(-x)))


def _dot(a, b):
    return jnp.dot(a, b, preferred_element_type=F32)


def _dot_nt(a, b):
    return lax.dot_general(a, b, (((1,), (1,)), ((), ())), preferred_element_type=F32)


def _dot_tn(a, b):
    return lax.dot_general(a, b, (((0,), (0,)), ((), ())), preferred_element_type=F32)


def _rms_mod(x, gain, shift, scale):
    ms = jnp.mean(x * x, axis=-1, keepdims=True)
    y = x * lax.rsqrt(ms + EPS) * gain
    return y * (1.0 + scale) + shift


def _acc_cols(acc_ref, gate, lhs, w_ref, cn=512):
    for n in range(acc_ref.shape[1] // cn):
        cols = slice(n * cn, (n + 1) * cn)
        acc_ref[:, cols] += gate[:, cols] * _dot(lhs, w_ref[:, cols])


def _mod_part(m_ref, k):
    return m_ref[0, :, k * D_MODEL:(k + 1) * D_MODEL]


def _split3(v):
    hi = v.astype(BF16)
    r1 = v - hi.astype(F32)
    mid = r1.astype(BF16)
    lo = (r1 - mid.astype(F32)).astype(BF16)
    return hi, mid, lo


def _ada_kernel(c_ref, w_ref, b_ref, o_ref):
    s = _silu(c_ref[...]).astype(BF16)
    o_ref[...] = _dot(s, w_ref[...].astype(BF16)) + b_ref[...]


def _ada(cvec, w, b):
    n = w.shape[1]
    tn = 1024
    out = pl.pallas_call(
        _ada_kernel,
        grid=(n // tn,),
        in_specs=[pl.BlockSpec((8, D_MODEL), lambda j: (0, 0)),
                  pl.BlockSpec((D_MODEL, tn), lambda j: (0, j)),
                  pl.BlockSpec((1, tn), lambda j: (0, j))],
        out_specs=pl.BlockSpec((8, tn), lambda j: (0, j)),
        out_shape=jax.ShapeDtypeStruct((8, n), F32),
        compiler_params=_cparams("arbitrary"),
        name="ada",
    )(cvec, w, b.reshape(1, n))
    return out.reshape(8, 1, n)


def _prep_kernel(x_ref, c_ref, pos_ref, g_ref, m_ref, xr_ref, h_ref, *, n_lat):
    i = pl.program_id(0)
    sh, sc = _mod_part(m_ref, 0), _mod_part(m_ref, 1)

    def emit(xv):
        xr_ref[...] = xv
        h_ref[...] = _rms_mod(xv, g_ref[...], sh, sc).astype(BF16)

    @pl.when(i < n_lat)
    def _():
        emit(x_ref[...] + pos_ref[...])

    @pl.when(i >= n_lat)
    def _():
        emit(c_ref[...])


def _mod_row_map(n_lat, tiles_per_batch, ctx_row):
    def row(i):
        return jnp.where(i < n_lat, i // tiles_per_batch, ctx_row)
    return row


def _prep(x2, c2, pos, gain, mods, batch, seq):
    tm = TM_DENSE
    n_lat, n_ctx = x2.shape[0] // tm, c2.shape[0] // tm
    tpb = seq // tm
    row = _mod_row_map(n_lat, tpb, batch)
    t = x2.shape[0] + c2.shape[0]
    return pl.pallas_call(
        functools.partial(_prep_kernel, n_lat=n_lat),
        grid=(n_lat + n_ctx,),
        in_specs=[pl.BlockSpec((tm, D_MODEL), lambda i: (jnp.minimum(i, n_lat - 1), 0)),
                  pl.BlockSpec((tm, D_MODEL), lambda i: (jnp.maximum(i - n_lat, 0), 0)),
                  pl.BlockSpec((tm, D_MODEL), lambda i: (i % tpb, 0)),
                  pl.BlockSpec((1, D_MODEL), lambda i: (0, 0)),
                  pl.BlockSpec((1, 1, N_MOD * D_MODEL), lambda i: (row(i), 0, 0))],
        out_specs=[pl.BlockSpec((tm, D_MODEL), lambda i: (i, 0)),
                   pl.BlockSpec((tm, D_MODEL), lambda i: (i, 0))],
        out_shape=[jax.ShapeDtypeStruct((t, D_MODEL), F32),
                   jax.ShapeDtypeStruct((t, D_MODEL), BF16)],
        compiler_params=_cparams("arbitrary"),
        name="prep",
    )(x2, c2, pos, gain, mods)


def _gmlp_in_kernel(h_ref, w_ref, b_ref, z_ref, st_ref, *, n_u, tn, cw):
    j = pl.program_id(1)
    h = h_ref[...]
    tm = h.shape[0]
    s1 = jnp.zeros((tm, LANES), F32)
    s2 = jnp.zeros((tm, LANES), F32)
    for c in range(tn // cw):
        a = _dot(h, w_ref[:, c * cw:(c + 1) * cw].astype(BF16)) + b_ref[:, c * cw:(c + 1) * cw]
        g = 0.5 * a * (1.0 + lax.erf(a * INV_SQRT2))
        z_ref[:, c * cw:(c + 1) * cw] = g.astype(BF16)
        for q in range(cw // LANES):
            gq = g[:, q * LANES:(q + 1) * LANES]
            s1 = s1 + gq
            s2 = s2 + gq * gq

    @pl.when(j == n_u)
    def _():
        st_ref[:, 0:LANES] = s1
        st_ref[:, LANES:2 * LANES] = s2

    @pl.when(j > n_u)
    def _():
        st_ref[:, 0:LANES] += s1
        st_ref[:, LANES:2 * LANES] += s2


def _gmlp_in(h, w, b):
    t = h.shape[0]
    tm, tn = TM_WIDE, 1024
    n = w.shape[1]
    return pl.pallas_call(
        functools.partial(_gmlp_in_kernel, n_u=GMLP_HALF // tn, tn=tn, cw=256),
        grid=(t // tm, n // tn),
        in_specs=[pl.BlockSpec((tm, D_MODEL), lambda i, j: (i, 0)),
                  pl.BlockSpec((D_MODEL, tn), lambda i, j: (0, j)),
                  pl.BlockSpec((1, tn), lambda i, j: (0, j))],
        out_specs=[pl.BlockSpec((tm, tn), lambda i, j: (i, j)),
                   pl.BlockSpec((tm, 2 * LANES), lambda i, j: (i, 0))],
        out_shape=[jax.ShapeDtypeStruct((t, n), BF16),
                   jax.ShapeDtypeStruct((t, 2 * LANES), F32)],
        compiler_params=_cparams("arbitrary", "arbitrary"),
        name="gmlp_in",
    )(h, w, b.reshape(1, n))


def _sgu_out_kernel(zu_ref, zv_ref, st_ref, lg_ref, lb_ref, ws_ref, bs_ref, wo_ref, xr_ref,
                    m_ref, gn_ref, x1_ref, hn_ref, gs_ref, *, n_k, gps):
    k = pl.program_id(1)
    tm = zu_ref.shape[0]
    st = st_ref[...]
    mu = jnp.sum(st[:, 0:LANES], axis=-1, keepdims=True) * (1.0 / GMLP_HALF)
    ex2 = jnp.sum(st[:, LANES:2 * LANES], axis=-1, keepdims=True) * (1.0 / GMLP_HALF)
    rstd = lax.rsqrt(ex2 - mu * mu + EPS)
    for g in range(gps):
        cols = slice(g * GMLP_GC, (g + 1) * GMLP_GC)
        ws = ws_ref[g].astype(BF16)
        bs = bs_ref[g]
        for r in range(tm // CHUNK):
            rows = slice(r * CHUNK, (r + 1) * CHUNK)
            v = zv_ref[rows, cols].astype(F32)
            vn = (v - mu[rows]) * rstd[rows] * lg_ref[:, cols] + lb_ref[:, cols]
            s = _dot(ws, vn.astype(BF16)) + bs
            gs_ref[rows, cols] = (zu_ref[rows, cols].astype(F32) * s).astype(BF16)

    @pl.when(k == 0)
    def _():
        x1_ref[...] = xr_ref[...]

    _acc_cols(x1_ref, _mod_part(m_ref, 2), gs_ref[...], wo_ref)

    @pl.when(k == n_k - 1)
    def _():
        hn_ref[...] = _rms_mod(x1_ref[...], gn_ref[...], _mod_part(m_ref, 3),
                               _mod_part(m_ref, 4)).astype(BF16)


def _sgu_out(z, st, lg, lb, ws, bs, wo, xr, mods, gain_next, row):
    t = xr.shape[0]
    tm, gps = TM_DENSE, 2
    n_k, tk = GMLP_GROUPS // gps, gps * GMLP_GC
    return pl.pallas_call(
        functools.partial(_sgu_out_kernel, n_k=n_k, gps=gps),
        grid=(t // tm, n_k),
        in_specs=[pl.BlockSpec((tm, tk), lambda i, k: (i, k)),
                  pl.BlockSpec((tm, tk), lambda i, k: (i, n_k + k)),
                  pl.BlockSpec((tm, 2 * LANES), lambda i, k: (i, 0)),
                  pl.BlockSpec((1, tk), lambda i, k: (0, k)),
                  pl.BlockSpec((1, tk), lambda i, k: (0, k)),
                  pl.BlockSpec((gps, CHUNK, CHUNK), lambda i, k: (k, 0, 0)),
                  pl.BlockSpec((gps, CHUNK, 1), lambda i, k: (k, 0, 0)),
                  pl.BlockSpec((tk, D_MODEL), lambda i, k: (k, 0)),
                  pl.BlockSpec((tm, D_MODEL), lambda i, k: (i, 0)),
                  pl.BlockSpec((1, 1, N_MOD * D_MODEL), lambda i, k: (row(i), 0, 0)),
                  pl.BlockSpec((1, D_MODEL), lambda i, k: (0, 0))],
        out_specs=[pl.BlockSpec((tm, D_MODEL), lambda i, k: (i, 0)),
                   pl.BlockSpec((tm, D_MODEL), lambda i, k: (i, 0))],
        out_shape=[jax.ShapeDtypeStruct((t, D_MODEL), F32),
                   jax.ShapeDtypeStruct((t, D_MODEL), BF16)],
        scratch_shapes=[pltpu.VMEM((tm, tk), BF16)],
        compiler_params=_cparams("arbitrary", "arbitrary"),
        name="sgu_out",
    )(z, z, st, lg, lb, ws, bs, wo, xr, mods, gain_next)


def _ffn_kernel(h_ref, wg_ref, wu_ref, wd_ref, x1_ref, m_ref, mn_ref, gn_ref, x2_ref, hn_ref, *, n_f):
    f = pl.program_id(1)

    @pl.when(f == 0)
    def _():
        x2_ref[...] = x1_ref[...]

    h = h_ref[...]
    tf = wg_ref.shape[1]
    for sl in range(2):
        cols = slice(sl * tf // 2, (sl + 1) * tf // 2)
        a = (_silu(_dot(h, wg_ref[:, cols])) * _dot(h, wu_ref[:, cols])).astype(BF16)
        _acc_cols(x2_ref, _mod_part(m_ref, 5), a, wd_ref.at[cols, :])

    @pl.when(f == n_f - 1)
    def _():
        hn_ref[...] = _rms_mod(x2_ref[...], gn_ref[...], _mod_part(mn_ref, 0),
                               _mod_part(mn_ref, 1)).astype(BF16)


def _ffn(h, wgu, wd, x1, mods, mods_next, gain_next, row):
    t = h.shape[0]
    tm, tf = TM_DENSE, 512
    n_f = FFN_DFF // tf
    mspec = pl.BlockSpec((1, 1, N_MOD * D_MODEL), lambda i, f: (row(i), 0, 0))
    return pl.pallas_call(
        functools.partial(_ffn_kernel, n_f=n_f),
        grid=(t // tm, n_f),
        in_specs=[pl.BlockSpec((tm, D_MODEL), lambda i, f: (i, 0)),
                  pl.BlockSpec((D_MODEL, tf), lambda i, f: (0, f)),
                  pl.BlockSpec((D_MODEL, tf), lambda i, f: (0, n_f + f)),
                  pl.BlockSpec((tf, D_MODEL), lambda i, f: (f, 0)),
                  pl.BlockSpec((tm, D_MODEL), lambda i, f: (i, 0)),
                  mspec, mspec,
                  pl.BlockSpec((1, D_MODEL), lambda i, f: (0, 0))],
        out_specs=[pl.BlockSpec((tm, D_MODEL), lambda i, f: (i, 0)),
                   pl.BlockSpec((tm, D_MODEL), lambda i, f: (i, 0))],
        out_shape=[jax.ShapeDtypeStruct((t, D_MODEL), F32),
                   jax.ShapeDtypeStruct((t, D_MODEL), BF16)],
        compiler_params=_cparams("arbitrary", "arbitrary"),
        name="ffn",
    )(h, wgu, wgu, wd, x1, mods, mods_next, gain_next)


def _proj_kernel(h_ref, w_ref, o_ref, *, tn, cw):
    h = h_ref[...]
    for c in range(tn // cw):
        o_ref[:, c * cw:(c + 1) * cw] = _dot(
            h, w_ref[:, c * cw:(c + 1) * cw].astype(BF16)).astype(o_ref.dtype)


def _dt_kernel(h_ref, w_ref, b_ref, o_ref):
    a = _dot(h_ref[...], w_ref[...].astype(BF16)) + b_ref[...]
    o_ref[...] = jnp.maximum(a, 0.0) + jnp.log(1.0 + jnp.exp(-jnp.abs(a)))


def _ssd_in(h, w, dt_bias):
    t = h.shape[0]
    tm, tn = TM_WIDE, 1024
    zx = pl.pallas_call(
        functools.partial(_proj_kernel, tn=tn, cw=256),
        grid=(t // tm, SSD_ZX // tn),
        in_specs=[pl.BlockSpec((tm, D_MODEL), lambda i, j: (i, 0)),
                  pl.BlockSpec((D_MODEL, tn), lambda i, j: (0, j))],
        out_specs=pl.BlockSpec((tm, tn), lambda i, j: (i, j)),
        out_shape=jax.ShapeDtypeStruct((t, SSD_ZX), BF16),
        compiler_params=_cparams("arbitrary", "arbitrary"),
        name="ssd_in",
    )(h, w)
    nd = 2 * SSD_HEADS
    dt = pl.pallas_call(
        _dt_kernel,
        grid=(t // tm,),
        in_specs=[pl.BlockSpec((tm, D_MODEL), lambda i: (i, 0)),
                  pl.BlockSpec((D_MODEL, nd), lambda i: (0, SSD_ZX // nd)),
                  pl.BlockSpec((1, nd), lambda i: (0, 0))],
        out_specs=pl.BlockSpec((tm, nd), lambda i: (i, 0)),
        out_shape=jax.ShapeDtypeStruct((t, nd), F32),
        compiler_params=_cparams("arbitrary"),
        name="ssd_dt",
    )(h, w, dt_bias.reshape(1, nd))
    return zx, dt


def _conv_kernel(z_ref, w_ref, b_ref, o_ref, pad_ref, *, seq):
    tc = z_ref.shape[1]
    halo = 8
    pad_ref[0:halo, :] = jnp.zeros((halo, tc), F32)
    pad_ref[halo:halo + seq, :] = z_ref[...].astype(F32)
    pad_ref[halo + seq:2 * halo + seq, :] = jnp.zeros((halo, tc), F32)
    acc = jnp.zeros((seq, tc), F32) + b_ref[...]
    for k in range(SSD_CONV):
        off = halo - SSD_CONV // 2 + k
        acc = acc + pad_ref[off:off + seq, :] * w_ref[k:k + 1, :]
    o_ref[...] = _silu(acc).astype(BF16)


def _conv(zx, w, b, batch, seq, row0):
    tc = 256
    nblk = SSD_CONV_CH // tc
    off_r, off_c = row0 // seq, SSD_INNER // tc
    return pl.pallas_call(
        functools.partial(_conv_kernel, seq=seq),
        grid=(batch, nblk),
        in_specs=[pl.BlockSpec((seq, tc), lambda bi, c: (off_r + bi, off_c + c)),
                  pl.BlockSpec((SSD_CONV, tc), lambda bi, c: (0, c)),
                  pl.BlockSpec((1, tc), lambda bi, c: (0, c))],
        out_specs=pl.BlockSpec((seq, tc), lambda bi, c: (bi, c)),
        out_shape=jax.ShapeDtypeStruct((batch * seq, SSD_CONV_CH), BF16),
        scratch_shapes=[pltpu.VMEM((seq + 16, tc), F32)],
        compiler_params=_cparams("arbitrary", "arbitrary"),
        name="ssd_conv",
    )(zx, w, b.reshape(1, SSD_CONV_CH))


def _cumsum_rows(a, ltri3_ref):
    hi, mid, lo = _split3(a)
    return _dot(ltri3_ref[...], jnp.concatenate([hi, mid, lo], axis=0))


def _expand_heads(v, r3_ref, off):
    pieces = [p.astype(F32)[:, off:off + SSD_HEADS] for p in _split3(v)]
    pieces.append(jnp.zeros_like(pieces[0]))
    return _dot(jnp.concatenate(pieces, axis=1).astype(BF16), r3_ref[...])


def _state_update(s_ref, x_f32, bmat, w1x, decay_row):
    xw = (x_f32 * w1x).astype(BF16)
    for g in range(SSD_GROUPS):
        gs = slice(g * SSD_GW, (g + 1) * SSD_GW)
        bg = bmat[:, g * SSD_STATE:(g + 1) * SSD_STATE]
        s_ref[g] = s_ref[g] * decay_row[:, gs] + _dot_tn(bg, xw[:, gs])


def _ssd_fwd_kernel(xl_ref, bl_ref, cl_ref, xc_ref, bc_ref, dt_ref, alog_ref, dsk_ref, ltri_ref,
                    r3_ref, y_ref, s_ref, *, n_ctx):
    s = pl.program_id(1)
    q = SSD_CHUNK

    @pl.when(s == 0)
    def _():
        s_ref[...] = jnp.zeros(s_ref.shape, F32)

    dt = dt_ref[...]
    a = dt * (-jnp.exp(alog_ref[...]) * LOG2E)
    cs = _cumsum_rows(a, ltri_ref)
    cs_end = cs[q - 1:q, :]
    w1 = dt * jnp.exp2(cs_end - cs)
    ex = _expand_heads(jnp.concatenate([jnp.exp2(cs), w1], axis=0), r3_ref, 0)
    ef, w1x = ex[0:q], ex[q:2 * q]

    @pl.when(s < n_ctx)
    def _():
        _state_update(s_ref, xc_ref[...].astype(F32), bc_ref[...], w1x, ef[q - 1:q])

    @pl.when(s >= n_ctx)
    def _():
        ce = cs - a
        ii = lax.broadcasted_iota(I32, (q, q), 0)
        jj = lax.broadcasted_iota(I32, (q, q), 1)
        lower, upper = ii >= jj, ii <= jj
        lane = lax.broadcasted_iota(I32, (q, LANES), 1)
        ldt = jnp.log2(dt)
        rows = jnp.where(lane < SSD_HEADS, cs - ldt, ce + ldt).T
        x = xl_ref[...]
        bmat, cmat = bl_ref[...], cl_ref[...]
        xf = x.astype(F32)
        nh = SSD_HEADS
        for g in range(SSD_GROUPS):
            cg = cmat[:, g * SSD_STATE:(g + 1) * SSD_STATE]
            gmat = _dot_nt(cg, bmat[:, g * SSD_STATE:(g + 1) * SSD_STATE])
            yoff = _dot(cg, s_ref[g].astype(BF16)) * ef[:, g * SSD_GW:(g + 1) * SSD_GW]
            for pr in range(SSD_RPG // 2):
                col = g * SSD_GW + pr * LANES
                xp = x[:, col:col + LANES]
                prod = []
                for half in range(2):
                    h = g * SSD_RPG + pr * 2 + half
                    mf = jnp.exp2(jnp.where(lower, cs[:, h:h + 1] - rows[h:h + 1, :], NEG_BIG))
                    mb = jnp.exp2(jnp.where(upper, rows[nh + h:nh + h + 1, :] - ce[:, nh + h:nh + h + 1],
                                            NEG_BIG))
                    prod.append(_dot(((mf + mb) * gmat).astype(BF16), xp))
                y_ref[:, col:col + LANES] = (
                    jnp.where(lane < SSD_HEADDIM, prod[0], prod[1]) + yoff[:, pr * LANES:(pr + 1) * LANES]
                    + dsk_ref[:, col:col + LANES] * xf[:, col:col + LANES])
        _state_update(s_ref, xf, bmat, w1x, ef[q - 1:q])


def _ssd_bwd_kernel(xl_ref, bl_ref, cl_ref, xc_ref, bc_ref, dt_ref, alog_ref, ltri_ref, r3_ref,
                    y1_ref, z_ref, ng_ref, o_ref, s_ref, *, n_ctx):
    s = pl.program_id(1)
    q = SSD_CHUNK

    @pl.when(s == 0)
    def _():
        s_ref[...] = jnp.zeros(s_ref.shape, F32)

    dt = dt_ref[...]
    a = dt * (-jnp.exp(alog_ref[...]) * LOG2E)
    cs = _cumsum_rows(a, ltri_ref)
    ce = cs - a
    cs_end = cs[q - 1:q, :]
    w1 = dt * jnp.exp2(ce)
    ex = _expand_heads(jnp.concatenate([jnp.exp2(cs_end - ce), w1], axis=0), r3_ref, SSD_HEADS)
    eb, w1x = ex[0:q], ex[q:2 * q]

    @pl.when(s < n_ctx)
    def _():
        _state_update(s_ref, xc_ref[...].astype(F32), bc_ref[...], w1x, eb[0:1])

    @pl.when(s >= n_ctx)
    def _():
        x = xl_ref[...]
        for g in range(SSD_GROUPS):
            gs = slice(g * SSD_GW, (g + 1) * SSD_GW)
            yoff = _dot(cl_ref[:, g * SSD_STATE:(g + 1) * SSD_STATE], s_ref[g].astype(BF16)) * eb[:, gs]
            yf = (y1_ref[:, gs] + yoff) * _silu(z_ref[:, gs].astype(F32))
            ms = jnp.mean(yf * yf, axis=-1, keepdims=True)
            o_ref[:, gs] = (yf * lax.rsqrt(ms + EPS) * ng_ref[:, gs]).astype(BF16)
        _state_update(s_ref, x.astype(F32), bl_ref[...], w1x, eb[0:1])


def _ssd(xbc_l, xbc_c, dt, zx, a_log, d_skip, norm_g, batch, seq, seq_c):
    q = SSD_CHUNK
    n_lat, n_ctx = seq // q, seq_c // q
    steps = n_ctx + n_lat
    t_lat = batch * seq
    nd = 2 * SSD_HEADS
    cb = SSD_INNER // SSD_BC
    tri = (jnp.arange(q)[:, None] >= jnp.arange(q)[None, :]).astype(BF16)
    ltri3 = jnp.concatenate([tri, tri, tri], axis=1)
    chan_head = jnp.arange(SSD_INNER) // SSD_HEADDIM
    r1 = (jnp.arange(SSD_HEADS)[:, None] == chan_head[None, :]).astype(BF16)
    r3 = jnp.concatenate([r1, r1, r1, jnp.zeros_like(r1)], axis=0)
    alog = a_log.reshape(1, nd)
    dsk = jnp.repeat(d_skip, SSD_HEADDIM).reshape(1, SSD_INNER)

    def lat_f(s):
        return jnp.maximum(s - n_ctx, 0)

    def ctx_f(s):
        return jnp.minimum(s, n_ctx - 1)

    def lat_b(s):
        return n_lat - 1 - jnp.maximum(s - n_ctx, 0)

    def ctx_b(s):
        return n_ctx - 1 - jnp.minimum(s, n_ctx - 1)

    def specs(lat, ctx):
        def dt_row(b, s):
            return jnp.where(s < n_ctx, t_lat // q + b * n_ctx + ctx(s), b * n_lat + lat(s))
        return [pl.BlockSpec((q, SSD_INNER), lambda b, s: (b * n_lat + lat(s), 0)),
                pl.BlockSpec((q, SSD_BC), lambda b, s: (b * n_lat + lat(s), cb)),
                pl.BlockSpec((q, SSD_BC), lambda b, s: (b * n_lat + lat(s), cb + 1)),
                pl.BlockSpec((q, SSD_INNER), lambda b, s: (b * n_ctx + ctx(s), 0)),
                pl.BlockSpec((q, SSD_BC), lambda b, s: (b * n_ctx + ctx(s), cb)),
                pl.BlockSpec((q, nd), lambda b, s: (dt_row(b, s), 0)),
                pl.BlockSpec((1, nd), lambda b, s: (0, 0))]

    const = lambda shape: pl.BlockSpec(shape, lambda b, s: (0,) * len(shape))
    state = pltpu.VMEM((SSD_GROUPS, SSD_STATE, SSD_GW), F32)

    y1 = pl.pallas_call(
        functools.partial(_ssd_fwd_kernel, n_ctx=n_ctx),
        grid=(batch, steps),
        in_specs=specs(lat_f, ctx_f) + [const((1, SSD_INNER)), const((q, 3 * q)),
                                        const((4 * SSD_HEADS, SSD_INNER))],
        out_specs=pl.BlockSpec((q, SSD_INNER), lambda b, s: (b * n_lat + lat_f(s), 0)),
        out_shape=jax.ShapeDtypeStruct((t_lat, SSD_INNER), F32),
        scratch_shapes=[state],
        compiler_params=_cparams("arbitrary", "arbitrary"),
        name="ssd_fwd",
    )(xbc_l, xbc_l, xbc_l, xbc_c, xbc_c, dt, alog, dsk, ltri3, r3)

    yn = pl.pallas_call(
        functools.partial(_ssd_bwd_kernel, n_ctx=n_ctx),
        grid=(batch, steps),
        in_specs=specs(lat_b, ctx_b) + [
            const((q, 3 * q)), const((4 * SSD_HEADS, SSD_INNER)),
            pl.BlockSpec((q, SSD_INNER), lambda b, s: (b * n_lat + lat_b(s), 0)),
            pl.BlockSpec((q, SSD_INNER), lambda b, s: (b * n_lat + lat_b(s), 0)),
            const((1, SSD_INNER))],
        out_specs=pl.BlockSpec((q, SSD_INNER), lambda b, s: (b * n_lat + lat_b(s), 0)),
        out_shape=jax.ShapeDtypeStruct((t_lat, SSD_INNER), BF16),
        scratch_shapes=[state],
        compiler_params=_cparams("arbitrary", "arbitrary"),
        name="ssd_bwd",
    )(xbc_l, xbc_l, xbc_l, xbc_c, xbc_c, dt, alog, ltri3, r3, y1, zx,
      norm_g.reshape(1, SSD_INNER))
    return yn


def _ssd_out_kernel(y_ref, w_ref, x2_ref, m_ref, gn_ref, rw_ref, x3_ref, hp_ref, rt_ref, *, n_k):
    k = pl.program_id(1)
    tm = x3_ref.shape[0]

    @pl.when(k == 0)
    def _():
        x3_ref[...] = x2_ref[...]

    _acc_cols(x3_ref, _mod_part(m_ref, 2), y_ref[...], w_ref)

    @pl.when(k == n_k - 1)
    def _():
        hm = _rms_mod(x3_ref[...], gn_ref[...], _mod_part(m_ref, 3), _mod_part(m_ref, 4))
        bits = pltpu.bitcast(hm.astype(BF16).astype(F32), jnp.uint32)
        for s in range(PACK_ROWS):
            lo = bits[:, s * LANES:(s + 1) * LANES] >> 16
            hi = bits[:, D_MODEL // 2 + s * LANES:D_MODEL // 2 + (s + 1) * LANES]
            hp_ref[pl.ds(s, tm, stride=PACK_ROWS), :] = lo | hi
        logits = jnp.dot(hm, rw_ref[...], preferred_element_type=F32,
                         precision=lax.Precision.HIGHEST)
        lane = lax.broadcasted_iota(I32, logits.shape, 1)
        lg = jnp.where(lane < N_EXPERTS, logits, -jnp.inf)
        m1 = jnp.max(lg, axis=-1, keepdims=True)
        i1 = jnp.min(jnp.where(lg == m1, lane, LANES), axis=-1, keepdims=True)
        lg2 = jnp.where(lane == i1, -jnp.inf, lg)
        m2 = jnp.max(lg2, axis=-1, keepdims=True)
        i2 = jnp.min(jnp.where(lg2 == m2, lane, LANES), axis=-1, keepdims=True)
        e2 = jnp.exp(m2 - m1)
        inv = 1.0 / (1.0 + e2)
        rt_ref[...] = jnp.where(lane == 0, i1.astype(F32),
                                jnp.where(lane == 1, i2.astype(F32),
                                          jnp.where(lane == 2, inv, e2 * inv)))


def _ssd_out(yn, w, x2, mods, gain, rw, row):
    t = yn.shape[0]
    tm, tk = TM_DENSE, 1024
    n_k = SSD_INNER // tk
    return pl.pallas_call(
        functools.partial(_ssd_out_kernel, n_k=n_k),
        grid=(t // tm, n_k),
        in_specs=[pl.BlockSpec((tm, tk), lambda i, k: (i, k)),
                  pl.BlockSpec((tk, D_MODEL), lambda i, k: (k, 0)),
                  pl.BlockSpec((tm, D_MODEL), lambda i, k: (i, 0)),
                  pl.BlockSpec((1, 1, N_MOD * D_MODEL), lambda i, k: (row(i), 0, 0)),
                  pl.BlockSpec((1, D_MODEL), lambda i, k: (0, 0)),
                  pl.BlockSpec((D_MODEL, LANES), lambda i, k: (0, 0))],
        out_specs=[pl.BlockSpec((tm, D_MODEL), lambda i, k: (i, 0)),
                   pl.BlockSpec((tm * PACK_ROWS, LANES), lambda i, k: (i, 0)),
                   pl.BlockSpec((tm, LANES), lambda i, k: (i, 0))],
        out_shape=[jax.ShapeDtypeStruct((t, D_MODEL), F32),
                   jax.ShapeDtypeStruct((t * PACK_ROWS, LANES), jnp.uint32),
                   jax.ShapeDtypeStruct((t, LANES), F32)],
        compiler_params=_cparams("arbitrary", "arbitrary"),
        name="ssd_out",
    )(yn, w, x2, mods, gain, rw)


def _row_copy(src_hbm, dst_ref, src_row, dst_row, sem):
    return pltpu.make_async_copy(src_hbm.at[pl.ds(src_row, 1), :], dst_ref.at[pl.ds(dst_row, 1), :], sem)


def _moe_kernel(src_ref, te_ref, nr_ref, nv_ref, hp_hbm, wg_ref, wu_ref, wd_ref, y_ref, xbuf_ref, x_ref,
                sem, *, n_tiles, n_f, cn):
    i, f = pl.program_id(0), pl.program_id(1)
    slot = i % 2
    nrows = nr_ref[i]
    half = TM_MOE // 2

    def row_copy(tile, r, slot_):
        src = pl.multiple_of(src_ref[tile * TM_MOE + r] * PACK_ROWS, PACK_ROWS)
        dst = pl.multiple_of(r * PACK_ROWS, PACK_ROWS)
        return pltpu.make_async_copy(hp_hbm.at[pl.ds(src, PACK_ROWS), :],
                                     xbuf_ref.at[slot_, pl.ds(dst, PACK_ROWS), :], sem.at[slot_])

    def wait_slot(slot_):
        pltpu.make_async_copy(hp_hbm.at[pl.ds(0, XBUF_ROWS * PACK_ROWS), :], xbuf_ref.at[slot_],
                              sem.at[slot_]).wait()

    @pl.when(jnp.logical_and(i == 0, f == 0))
    def _():
        def body(r, c):
            row_copy(0, r, 0).start()
            return c
        lax.fori_loop(0, XBUF_ROWS, body, 0)

    @pl.when(f == 0)
    def _():
        wait_slot(slot)
        xb = xbuf_ref.at[slot]
        for s in range(PACK_ROWS):
            w = xb[pl.ds(s, TM_MOE, stride=PACK_ROWS), :]
            x_ref[:, s * LANES:(s + 1) * LANES] = pltpu.bitcast(w << 16, F32).astype(BF16)
            x_ref[:, D_MODEL // 2 + s * LANES:D_MODEL // 2 + (s + 1) * LANES] = pltpu.bitcast(
                w & jnp.uint32(0xFFFF0000), F32).astype(BF16)
        y_ref[...] = jnp.zeros(y_ref.shape, F32)

    def prefetch_next():
        nxt = jnp.minimum(i + 1, n_tiles - 1)
        for k in range(PF_ROWS):
            row_copy(nxt, f * PF_ROWS + k, 1 - slot).start()

    def compute(rows):
        prefetch_next()
        x = x_ref[0:rows, :]
        a = (_silu(_dot(x, wg_ref[...].astype(BF16))) * _dot(x, wu_ref[...].astype(BF16))).astype(BF16)
        for n in range(D_MODEL // cn):
            y_ref[0:rows, n * cn:(n + 1) * cn] += _dot(a, wd_ref[:, n * cn:(n + 1) * cn].astype(BF16))

    @pl.when(nrows > half)
    def _():
        compute(TM_MOE)

    @pl.when(jnp.logical_and(nrows > 0, nrows <= half))
    def _():
        compute(half)

    @pl.when(nrows == 0)
    def _():
        prefetch_next()

    @pl.when(jnp.logical_and(i == n_tiles - 1, f == n_f - 1))
    def _():
        wait_slot(1 - slot)


def _moe(row_src, tile_expert, tile_rows, nvalid, hp, wgu, wd):
    n_tiles = tile_expert.shape[0]
    tm, tf, n_f = TM_MOE, TF_MOE, N_F_MOE

    def fmap(i, f, nv):
        return jnp.where(i < nv[0], f, n_f - 1)

    return pl.pallas_call(
        functools.partial(_moe_kernel, n_tiles=n_tiles, n_f=n_f, cn=512),
        grid_spec=pltpu.PrefetchScalarGridSpec(
            num_scalar_prefetch=4, grid=(n_tiles, n_f),
            in_specs=[pl.BlockSpec(memory_space=pl.ANY),
                      pl.BlockSpec((None, D_MODEL, tf),
                                   lambda i, f, src, te, nr, nv: (te[i], 0, fmap(i, f, nv))),
                      pl.BlockSpec((None, D_MODEL, tf),
                                   lambda i, f, src, te, nr, nv: (te[i], 0, n_f + fmap(i, f, nv))),
                      pl.BlockSpec((None, tf, D_MODEL),
                                   lambda i, f, src, te, nr, nv: (te[i], fmap(i, f, nv), 0))],
            out_specs=pl.BlockSpec((tm, D_MODEL), lambda i, f, src, te, nr, nv: (i, 0)),
            scratch_shapes=[pltpu.VMEM((2, XBUF_ROWS * PACK_ROWS, LANES), jnp.uint32),
                            pltpu.VMEM((tm, D_MODEL), BF16),
                            pltpu.SemaphoreType.DMA((2,))]),
        out_shape=jax.ShapeDtypeStruct((n_tiles * tm, D_MODEL), F32),
        compiler_params=_cparams("arbitrary", "arbitrary"),
        name="moe_ffn",
    )(row_src, tile_expert, tile_rows, nvalid, hp, wgu, wgu, wd)


def _combine_kernel(pos_ref, y_hbm, x3_ref, rt_ref, m_ref, gn_ref, o_ref, ya_ref, yb_ref, sem_a, sem_b):
    i = pl.program_id(0)
    tc = o_ref.shape[0]

    def issue(r, c):
        t2 = 2 * (i * tc + r)
        _row_copy(y_hbm, ya_ref, pos_ref[t2], r, sem_a).start()
        _row_copy(y_hbm, yb_ref, pos_ref[t2 + 1], r, sem_b).start()
        return c
    lax.fori_loop(0, tc, issue, 0, unroll=8)
    pltpu.make_async_copy(y_hbm.at[pl.ds(0, tc), :], ya_ref, sem_a).wait()
    pltpu.make_async_copy(y_hbm.at[pl.ds(0, tc), :], yb_ref, sem_b).wait()
    rt = rt_ref[...]
    moe = rt[:, 2:3] * ya_ref[...] + rt[:, 3:4] * yb_ref[...]
    x4 = x3_ref[...] + _mod_part(m_ref, 5) * moe
    ms = jnp.mean(x4 * x4, axis=-1, keepdims=True)
    o_ref[...] = x4 * lax.rsqrt(ms + EPS) * gn_ref[...]


def _combine(pos, y, x3, route, mods, gain, seq):
    t = x3.shape[0]
    tpb = seq // TC
    return pl.pallas_call(
        _combine_kernel,
        grid_spec=pltpu.PrefetchScalarGridSpec(
            num_scalar_prefetch=1, grid=(t // TC,),
            in_specs=[pl.BlockSpec(memory_space=pl.ANY),
                      pl.BlockSpec((TC, D_MODEL), lambda i, p: (i, 0)),
                      pl.BlockSpec((TC, LANES), lambda i, p: (i, 0)),
                      pl.BlockSpec((1, 1, N_MOD * D_MODEL), lambda i, p: (i // tpb, 0, 0)),
                      pl.BlockSpec((1, D_MODEL), lambda i, p: (0, 0))],
            out_specs=pl.BlockSpec((TC, D_MODEL), lambda i, p: (i, 0)),
            scratch_shapes=[pltpu.VMEM((TC, D_MODEL), F32), pltpu.VMEM((TC, D_MODEL), F32),
                            pltpu.SemaphoreType.DMA, pltpu.SemaphoreType.DMA]),
        out_shape=jax.ShapeDtypeStruct((t, D_MODEL), F32),
        compiler_params=_cparams("arbitrary"),
        name="moe_combine",
    )(pos, y, x3, route, mods, gain)


def _routing_tables(route, n_tok):
    e_flat = route[:, 0:TOP_K].astype(I32).reshape(-1)
    onehot = (e_flat[:, None] == jnp.arange(N_EXPERTS, dtype=I32)[None, :]).astype(I32)
    csum = jnp.cumsum(onehot, axis=0)
    counts = csum[-1]
    rank = jnp.take_along_axis(csum, e_flat[:, None], axis=1)[:, 0] - 1
    tiles = (counts + TM_MOE - 1) // TM_MOE
    tile_end = jnp.cumsum(tiles)
    pos = (tile_end - tiles)[e_flat] * TM_MOE + rank
    n_tiles = n_tok * TOP_K // TM_MOE + N_EXPERTS
    nvalid = tile_end[-1]
    row_src = jnp.zeros((n_tiles * TM_MOE + XBUF_ROWS - TM_MOE,), I32).at[pos].set(
        jnp.arange(n_tok * TOP_K, dtype=I32) // TOP_K)
    tid = jnp.arange(n_tiles, dtype=I32)
    tile_expert = jnp.minimum(jnp.searchsorted(tile_end, jnp.minimum(tid, nvalid - 1), side="right"),
                              N_EXPERTS - 1).astype(I32)
    local = tid - (tile_end - tiles)[tile_expert]
    tile_rows = jnp.where(tid < nvalid, jnp.clip(counts[tile_expert] - local * TM_MOE, 0, TM_MOE), 0)
    return pos.astype(I32), row_src, tile_expert, tile_rows.astype(I32), nvalid.reshape(1).astype(I32)


def _sincos_pos_2d(rows):
    quarter, half = D_MODEL // 4, D_MODEL // 2
    omega = 1.0 / (POS_BASE ** (jnp.arange(quarter, dtype=F32) / quarter))

    def emb1d(n):
        ang = jnp.arange(n, dtype=F32)[:, None] * omega[None, :]
        return jnp.concatenate([jnp.sin(ang), jnp.cos(ang)], axis=-1)

    er, ec = emb1d(rows), emb1d(GRID_W)
    grid = jnp.concatenate([jnp.broadcast_to(er[:, None, :], (rows, GRID_W, half)),
                            jnp.broadcast_to(ec[None, :, :], (rows, GRID_W, half))], axis=-1)
    return grid.reshape(rows * GRID_W, D_MODEL)


def kernel(x, c, ctx, c_ctx, l0_ada_w, l0_ada_b, l0_norm_mix, l0_gmlp_in_w, l0_gmlp_in_b, l0_sgu_ln_g, l0_sgu_ln_b, l0_sgu_ws, l0_sgu_bs, l0_gmlp_out_w, l0_norm_ffn, l0_ffn_w_gu, l0_ffn_w_down, l1_ada_w, l1_ada_b, l1_norm_mix, l1_ssd_in_w, l1_conv_w, l1_conv_b, l1_dt_bias, l1_a_log, l1_d_skip, l1_ssd_norm_g, l1_ssd_out_w, l1_norm_ffn, l1_router_w, l1_moe_w_gu, l1_moe_w_down, final_norm):
    batch, seq, _ = x.shape
    seq_c = ctx.shape[1]
    t_lat, t_ctx = batch * seq, batch * seq_c
    assert batch + 1 <= 8 and seq % TM_WIDE == 0 and t_ctx % TM_WIDE == 0 and seq_c % SSD_CHUNK == 0
    row = _mod_row_map(t_lat // TM_DENSE, seq // TM_DENSE, batch)
    vec = lambda g: g.reshape(1, -1)

    cvec = jnp.concatenate([c, c_ctx[None], jnp.zeros((8 - batch - 1, D_MODEL), F32)], axis=0)
    mods0 = _ada(cvec, l0_ada_w, l0_ada_b)
    mods1 = _ada(cvec, l1_ada_w, l1_ada_b)

    pos = _sincos_pos_2d(seq // GRID_W)
    xr, h = _prep(x.reshape(t_lat, D_MODEL), ctx.reshape(t_ctx, D_MODEL), pos, vec(l0_norm_mix),
                  mods0, batch, seq)
    z, st = _gmlp_in(h, l0_gmlp_in_w, l0_gmlp_in_b)
    x1, h = _sgu_out(z, st, vec(l0_sgu_ln_g), vec(l0_sgu_ln_b), l0_sgu_ws, l0_sgu_bs[:, :, None],
                     l0_gmlp_out_w.astype(BF16), xr, mods0, vec(l0_norm_ffn), row)
    x2, h = _ffn(h, l0_ffn_w_gu.astype(BF16), l0_ffn_w_down.astype(BF16), x1, mods0, mods1,
                 vec(l1_norm_mix), row)

    zx, dt = _ssd_in(h, l1_ssd_in_w, l1_dt_bias)
    xbc_l = _conv(zx, l1_conv_w, l1_conv_b, batch, seq, 0)
    xbc_c = _conv(zx, l1_conv_w, l1_conv_b, batch, seq_c, t_lat)
    yn = _ssd(xbc_l, xbc_c, dt, zx, l1_a_log, l1_d_skip, l1_ssd_norm_g, batch, seq, seq_c)
    rw = jnp.pad(l1_router_w, ((0, 0), (0, LANES - N_EXPERTS)))
    x3, hp, route = _ssd_out(yn, l1_ssd_out_w.astype(BF16), x2, mods1, vec(l1_norm_ffn), rw, row)

    pos_tab, row_src, tile_expert, tile_rows, nvalid = _routing_tables(route, t_lat)
    ys = _moe(row_src, tile_expert, tile_rows, nvalid, hp, l1_moe_w_gu, l1_moe_w_down)
    out = _combine(pos_tab, ys, x3, route, mods1, vec(final_norm), seq)
    return out.reshape(batch, seq, D_MODEL)
```
